```python
import math
import jax, jax.numpy as jnp
from jax import lax
import numpy as np

D_MODEL = 2048
BATCH = 4
SEQ = 2048
DEPTH = 2
DEC_BATCH = 8
DEC_SEQ = 4
PAST_LEN = 16384
PAGE_SIZE = 128

N_META = 16
D_ATTN = D_MODEL // 2
N_HEADS = 8
HEAD_DIM = D_ATTN // (2 * N_HEADS)
D_CONV = D_MODEL - D_ATTN
CONV_W = 31
D_FF = 4 * D_MODEL
D_IN = 3 * D_ATTN + 2 * D_CONV
Q_BLOCK = 128
EPS = 1e-6
NEG_INF = -1e30

kernel_name = "hymba_diffattn_conformer_step"


def lambda_init(l):
    return 0.8 - 0.6 * math.exp(-0.3 * l)


def rms_norm(x, g):
    xf = x.astype(jnp.float32)
    y = xf * lax.rsqrt(jnp.mean(xf * xf, axis=-1, keepdims=True) + EPS) * g.astype(jnp.float32)
    return y.astype(x.dtype)


def diff_lambda(l, lambda_q1, lambda_k1, lambda_q2, lambda_k2):
    f32 = jnp.float32
    return (jnp.exp(jnp.sum(lambda_q1[l].astype(f32) * lambda_k1[l].astype(f32)))
            - jnp.exp(jnp.sum(lambda_q2[l].astype(f32) * lambda_k2[l].astype(f32)))
            + lambda_init(l))


def diff_attn_core(q, k, v, q_pos, k_pos, lam):
    s = jnp.einsum('bqhd,bkhd->bhqk', q.astype(jnp.float32), k.astype(jnp.float32)) * (HEAD_DIM ** -0.5)
    s = jnp.where(k_pos[None, :] <= q_pos[:, None], s, NEG_INF)
    p = jax.nn.softmax(s, axis=-1)
    b, _, lq, lk = p.shape
    p = p.reshape(b, N_HEADS, 2, lq, lk)
    a = p[:, :, 0] - lam * p[:, :, 1]
    return jnp.einsum('bhqk,bkhe->bqhe', a, v.astype(jnp.float32))


def attend_prompt(q, k, v, lam):
    b, L = q.shape[:2]
    pos = jnp.arange(L)
    o_meta = diff_attn_core(q[:, :N_META], k[:, :N_META], v[:, :N_META], pos[:N_META], pos[:N_META], lam)
    nb = (L - N_META) // Q_BLOCK
    qb = q[:, N_META:].reshape(b, nb, Q_BLOCK, 2 * N_HEADS, HEAD_DIM).transpose(1, 0, 2, 3, 4)
    pb = pos[N_META:].reshape(nb, Q_BLOCK)
    ob = lax.map(lambda a: diff_attn_core(a[0], k, v, a[1], pos, lam), (qb, pb))
    o_real = ob.transpose(1, 0, 2, 3, 4).reshape(b, L - N_META, N_HEADS, 2 * HEAD_DIM)
    return jnp.concatenate([o_meta, o_real], axis=1)


def conformer_conv(u, prev, dw_w, dw_b, ln_g, ln_b):
    a, g = jnp.split(u, 2, axis=-1)
    h = a * jax.nn.sigmoid(g)
    ext = jnp.concatenate([prev.astype(h.dtype), h], axis=1)
    c = lax.conv_general_dilated(ext, dw_w[:, None, :].astype(ext.dtype), (1,), 'VALID',
                                 dimension_numbers=('NWC', 'WIO', 'NWC'),
                                 feature_group_count=D_CONV)
    cf = c.astype(jnp.float32) + dw_b.astype(jnp.float32)
    mu = jnp.mean(cf, axis=-1, keepdims=True)
    var = jnp.mean(jnp.square(cf - mu), axis=-1, keepdims=True)
    cn = (cf - mu) * lax.rsqrt(var + EPS) * ln_g.astype(jnp.float32) + ln_b.astype(jnp.float32)
    return jax.nn.silu(cn).astype(u.dtype), ext[:, -(CONV_W - 1):]


def project_in(x, l, attn_norm, w_in, q_norm, k_norm):
    b, L = x.shape[:2]
    z = rms_norm(x, attn_norm[l]) @ w_in[l]
    q = z[..., :D_ATTN].reshape(b, L, 2 * N_HEADS, HEAD_DIM)
    k = z[..., D_ATTN:2 * D_ATTN].reshape(b, L, 2 * N_HEADS, HEAD_DIM)
    v = z[..., 2 * D_ATTN:3 * D_ATTN].reshape(b, L, N_HEADS, 2 * HEAD_DIM)
    u = z[..., 3 * D_ATTN:]
    return rms_norm(q, q_norm[l]), rms_norm(k, k_norm[l]), v, u


def finish_layer(x, o_attn, c_out, l, attn_subln, w_out, mlp_norm, w_up, w_down):
    b, L = x.shape[:2]
    oa = (rms_norm(o_attn, attn_subln[l]) * (1.0 - lambda_init(l))).astype(x.dtype).reshape(b, L, D_ATTN)
    x = x + jnp.concatenate([oa, c_out], axis=-1) @ w_out[l]
    h = rms_norm(x, mlp_norm[l]) @ w_up[l]
    return x + jnp.square(jax.nn.relu(h)) @ w_down[l]


def setup_inputs(seed: int = 0) -> dict:
    key = jax.random.key(seed)
    ks = jax.random.split(key, 24)
    f32 = jnp.float32
    n_pages = PAST_LEN // PAGE_SIZE
    n_used = DEC_BATCH * n_pages
    n_phys = n_used + max(1, n_used // 4)
    nrm = lambda k, s, sc: jax.random.normal(k, s, f32) * sc
    page_table = jax.random.permutation(ks[5], n_phys)[:n_used].reshape(DEC_BATCH, n_pages).astype(jnp.int32)
    return {
        "x_prompt": nrm(ks[0], (BATCH, SEQ, D_MODEL), 1.0),
        "x_sample": nrm(ks[1], (DEC_BATCH, DEC_SEQ, D_MODEL), 1.0),
        "cache_k": nrm(ks[2], (DEPTH, n_phys, PAGE_SIZE, 2 * N_HEADS, HEAD_DIM), 1.0),
        "cache_v": nrm(ks[3], (DEPTH, n_phys, PAGE_SIZE, N_HEADS, 2 * HEAD_DIM), 1.0),
        "state_conv": nrm(ks[4], (DEPTH, DEC_BATCH, CONV_W - 1, D_CONV), 0.5),
        "page_table": page_table,
        "meta_tokens": nrm(ks[6], (N_META, D_MODEL), 1.0),
        "attn_norm": 1.0 + nrm(ks[7], (DEPTH, D_MODEL), 0.02),
        "w_in": nrm(ks[8], (DEPTH, D_MODEL, D_IN), D_MODEL ** -0.5),
        "q_norm": 1.0 + nrm(ks[9], (DEPTH, HEAD_DIM), 0.02),
        "k_norm": 1.0 + nrm(ks[10], (DEPTH, HEAD_DIM), 0.02),
        "lambda_q1": nrm(ks[11], (DEPTH, HEAD_DIM), 0.1),
        "lambda_k1": nrm(ks[12], (DEPTH, HEAD_DIM), 0.1),
        "lambda_q2": nrm(ks[13], (DEPTH, HEAD_DIM), 0.1),
        "lambda_k2": nrm(ks[14], (DEPTH, HEAD_DIM), 0.1),
        "attn_subln": 1.0 + nrm(ks[15], (DEPTH, 2 * HEAD_DIM), 0.02),
        "dw_w": nrm(ks[16], (DEPTH, CONV_W, D_CONV), CONV_W ** -0.5),
        "dw_b": nrm(ks[17], (DEPTH, D_CONV), 0.02),
        "conv_ln_g": 1.0 + nrm(ks[18], (DEPTH, D_CONV), 0.02),
        "conv_ln_b": nrm(ks[19], (DEPTH, D_CONV), 0.02),
        "w_out": nrm(ks[20], (DEPTH, D_ATTN + D_CONV, D_MODEL), (D_ATTN + D_CONV) ** -0.5),
        "mlp_norm": 1.0 + nrm(ks[21], (DEPTH, D_MODEL), 0.02),
        "w_up": nrm(ks[22], (DEPTH, D_MODEL, D_FF), D_MODEL ** -0.5),
        "w_down": nrm(ks[23], (DEPTH, D_FF, D_MODEL), D_FF ** -0.5),
    }


def reference(x_prompt, x_sample, cache_k, cache_v, state_conv, page_table, meta_tokens,
              attn_norm, w_in, q_norm, k_norm, lambda_q1, lambda_k1, lambda_q2, lambda_k2,
              attn_subln, dw_w, dw_b, conv_ln_g, conv_ln_b, w_out, mlp_norm, w_up, w_down):
    b = x_prompt.shape[0]
    meta = jnp.broadcast_to(meta_tokens.astype(x_prompt.dtype)[None], (b, N_META, D_MODEL))
    x = jnp.concatenate([meta, x_prompt], axis=1)
    kp, vp, cp = [], [], []
    for l in range(DEPTH):
        lam = diff_lambda(l, lambda_q1, lambda_k1, lambda_q2, lambda_k2)
        q, k, v, u = project_in(x, l, attn_norm, w_in, q_norm, k_norm)
        o = attend_prompt(q, k, v, lam)
        c, cs = conformer_conv(u, jnp.zeros((b, CONV_W - 1, D_CONV), u.dtype),
                               dw_w[l], dw_b[l], conv_ln_g[l], conv_ln_b[l])
        kp.append(k); vp.append(v); cp.append(cs)
        x = finish_layer(x, o, c, l, attn_subln, w_out, mlp_norm, w_up, w_down)
    y_prompt = x[:, N_META:]

    db, ds = x_sample.shape[:2]
    past_len = page_table.shape[1] * cache_k.shape[2]
    q_pos = past_len + jnp.arange(ds)
    k_pos = jnp.arange(past_len + ds)
    xs = x_sample
    ks_, vs_, cs_ = [], [], []
    for l in range(DEPTH):
        lam = diff_lambda(l, lambda_q1, lambda_k1, lambda_q2, lambda_k2)
        q, k, v, u = project_in(xs, l, attn_norm, w_in, q_norm, k_norm)
        k_past = cache_k[l][page_table].reshape(db, past_len, 2 * N_HEADS, HEAD_DIM)
        v_past = cache_v[l][page_table].reshape(db, past_len, N_HEADS, 2 * HEAD_DIM)
        k_all = jnp.concatenate([k_past.astype(k.dtype), k], axis=1)
        v_all = jnp.concatenate([v_past.astype(v.dtype), v], axis=1)
        o = diff_attn_core(q, k_all, v_all, q_pos, k_pos, lam)
        c, cs = conformer_conv(u, state_conv[l], dw_w[l], dw_b[l], conv_ln_g[l], conv_ln_b[l])
        ks_.append(k); vs_.append(v); cs_.append(cs)
        xs = finish_layer(xs, o, c, l, attn_subln, w_out, mlp_norm, w_up, w_down)
    y_sample = xs

    return (y_prompt, y_sample, jnp.stack(kp), jnp.stack(vp), jnp.stack(cp),
            jnp.stack(ks_), jnp.stack(vs_), jnp.stack(cs_))
```

```python
import functools
import math

import jax
import jax.numpy as jnp
from jax import lax
from jax.experimental import pallas as pl
from jax.experimental.pallas import tpu as pltpu

F32 = jnp.float32
BF16 = jnp.bfloat16

D_MODEL = 2048
N_META = 16
D_ATTN = D_MODEL // 2
N_HEADS = 8
HEAD_DIM = D_ATTN // (2 * N_HEADS)
PAIR_DIM = 2 * HEAD_DIM
D_CONV = D_MODEL - D_ATTN
CONV_W = 31
D_FF = 4 * D_MODEL
D_IN = 3 * D_ATTN + 2 * D_CONV
EPS = 1e-6
NEG_INF = -1e30
QK_SCALE = HEAD_DIM ** -0.5

VMEM_LIMIT_BYTES = 56 * 1024 * 1024

IN_CHUNK = 512
ATTN_TILE = 256
CONV_ROWS = 48
CONV_HALO = 32
PAGES_PER_STEP = 8


def lambda_init(l):
    return 0.8 - 0.6 * math.exp(-0.3 * l)


def _params(*semantics):
    return pltpu.CompilerParams(dimension_semantics=semantics, vmem_limit_bytes=VMEM_LIMIT_BYTES)


def _rms_rows(x, gain):
    ms = jnp.mean(x * x, axis=-1, keepdims=True)
    return x * lax.rsqrt(ms + EPS) * gain


def _diff_lambda(lq1, lk1, lq2, lk2, lam_init):
    e1 = jnp.exp(jnp.sum(lq1[...] * lk1[...], axis=-1, keepdims=True))
    e2 = jnp.exp(jnp.sum(lq2[...] * lk2[...], axis=-1, keepdims=True))
    return e1 - e2 + lam_init


def _in_proj_kernel(x_ref, g_ref, w_ref, wg_ref, gm_ref, qg_ref, kg_ref,
                    q_ref, k_ref, v_ref, h_ref, xn_ref):
    j = pl.program_id(1)
    n_q = D_ATTN // IN_CHUNK

    @pl.when(j == 0)
    def _():
        xn_ref[...] = _rms_rows(x_ref[...], g_ref[...]).astype(BF16)

    xn = xn_ref[...]
    z = jnp.dot(xn, w_ref[...], preferred_element_type=F32)

    def head_norm(gain):
        ms = jnp.dot((z * z).astype(BF16), gm_ref[...], preferred_element_type=F32)
        return z * lax.rsqrt(ms + EPS) * gain

    @pl.when(j < n_q)
    def _():
        q_ref[...] = (head_norm(qg_ref[...]) * QK_SCALE).astype(BF16)

    @pl.when((j >= n_q) & (j < 2 * n_q))
    def _():
        k_ref[...] = head_norm(kg_ref[...])

    @pl.when((j >= 2 * n_q) & (j < 3 * n_q))
    def _():
        v_ref[...] = z

    @pl.when(j >= 3 * n_q)
    def _():
        gate = jnp.dot(xn, wg_ref[...], preferred_element_type=F32)
        h_ref[...] = z * jax.nn.sigmoid(gate)


def _in_proj(x, gain, w_bf16, gm, qg, kg, tm):
    m = x.shape[0]
    n_q = D_ATTN // IN_CHUNK
    n_steps = 4 * n_q
    gate0 = (3 * D_ATTN + D_CONV) // IN_CHUNK

    def region(r):
        return lambda i, j: (i, jnp.clip(j - r * n_q, 0, n_q - 1))

    return pl.pallas_call(
        _in_proj_kernel,
        grid=(m // tm, n_steps),
        in_specs=[
            pl.BlockSpec((tm, D_MODEL), lambda i, j: (i, 0)),
            pl.BlockSpec((1, D_MODEL), lambda i, j: (0, 0)),
            pl.BlockSpec((D_MODEL, IN_CHUNK), lambda i, j: (0, j)),
            pl.BlockSpec((D_MODEL, IN_CHUNK), lambda i, j: (0, gate0 + jnp.maximum(j - 3 * n_q, 0))),
            pl.BlockSpec((IN_CHUNK, IN_CHUNK), lambda i, j: (0, 0)),
            pl.BlockSpec((1, IN_CHUNK), lambda i, j: (0, 0)),
            pl.BlockSpec((1, IN_CHUNK), lambda i, j: (0, 0)),
        ],
        out_specs=[
            pl.BlockSpec((tm, IN_CHUNK), region(0)),
            pl.BlockSpec((tm, IN_CHUNK), region(1)),
            pl.BlockSpec((tm, IN_CHUNK), region(2)),
            pl.BlockSpec((tm, IN_CHUNK), region(3)),
        ],
        out_shape=[
            jax.ShapeDtypeStruct((m, D_ATTN), BF16),
            jax.ShapeDtypeStruct((m, D_ATTN), F32),
            jax.ShapeDtypeStruct((m, D_ATTN), F32),
            jax.ShapeDtypeStruct((m, D_CONV), F32),
        ],
        scratch_shapes=[pltpu.VMEM((tm, D_MODEL), BF16)],
        compiler_params=_params("parallel", "arbitrary"),
        name="in_proj",
    )(x, gain, w_bf16, w_bf16, gm, qg, kg)


def _softmax_step(qq, kc, vc, carry, mask):
    m, l, acc = carry
    s = lax.dot_general(qq, kc, (((1,), (1,)), ((), ())), preferred_element_type=F32)
    if mask is not None:
        s = jnp.where(mask, s, NEG_INF)
    m_new = jnp.maximum(m, jnp.max(s, axis=-1, keepdims=True))
    alpha = jnp.exp(m - m_new)
    p = jnp.exp(s - m_new)
    l = alpha * l + jnp.sum(p, axis=-1, keepdims=True)
    acc = alpha * acc + jnp.dot(p.astype(BF16), vc, preferred_element_type=F32)
    return m_new, l, acc


def _sub_ln(o, gain, lam_init):
    ms = jnp.mean(o * o, axis=-1, keepdims=True)
    return o * lax.rsqrt(ms + EPS) * gain * (1.0 - lam_init)


def _attn_prompt_kernel(lq1, lk1, lq2, lk2, sg_ref, q_ref, k_ref, v_ref, o_ref,
                        kb_ref, vb_ref, *, seq, lam_init):
    lam = _diff_lambda(lq1, lk1, lq2, lk2, lam_init)
    pad = kb_ref.shape[0] - seq
    kb_ref[pl.ds(0, seq), :] = k_ref[0].astype(BF16)
    vb_ref[pl.ds(0, seq), :] = v_ref[0].astype(BF16)
    kb_ref[pl.ds(seq, pad), :] = jnp.zeros((pad, PAIR_DIM), BF16)
    vb_ref[pl.ds(seq, pad), :] = jnp.zeros((pad, PAIR_DIM), BF16)

    first = lax.broadcasted_iota(jnp.int32, (1, PAIR_DIM), 1) < HEAD_DIM
    n_tiles = pl.cdiv(seq, ATTN_TILE)
    for i in range(n_tiles):
        r0 = i * ATTN_TILE
        t = min(ATTN_TILE, seq - r0)
        qt = q_ref[0, pl.ds(r0, t), :]
        zero = jnp.zeros_like(qt)
        qq = jnp.concatenate([jnp.where(first, qt, zero), jnp.where(first, zero, qt)], axis=0)
        carry = (jnp.full((2 * t, 1), NEG_INF, F32), jnp.zeros((2 * t, 1), F32),
                 jnp.zeros((2 * t, PAIR_DIM), F32))

        def body(c, carry, qq=qq):
            c0 = pl.multiple_of(c * ATTN_TILE, ATTN_TILE)
            return _softmax_step(qq, kb_ref[pl.ds(c0, ATTN_TILE), :],
                                 vb_ref[pl.ds(c0, ATTN_TILE), :], carry, None)

        if i > 0:
            carry = lax.fori_loop(0, i, body, carry)
        qpos = lax.broadcasted_iota(jnp.int32, (2 * t, ATTN_TILE), 0) % t
        kpos = lax.broadcasted_iota(jnp.int32, (2 * t, ATTN_TILE), 1)
        carry = _softmax_step(qq, kb_ref[pl.ds(r0, ATTN_TILE), :], vb_ref[pl.ds(r0, ATTN_TILE), :],
                              carry, kpos <= qpos)
        _, l, acc = carry
        o = acc[:t] / l[:t] - lam * (acc[t:] / l[t:])
        o_ref[0, pl.ds(r0, t), :] = _sub_ln(o, sg_ref[...], lam_init).astype(BF16)


def _attn_prompt(lams, sub_gain, q, k, v, batch, seq, lam_init):
    q3 = q.reshape(batch, seq, D_ATTN)
    k3 = k.reshape(batch, seq, D_ATTN)
    v3 = v.reshape(batch, seq, D_ATTN)
    seq_pad = pl.cdiv(seq, ATTN_TILE) * ATTN_TILE
    vec = pl.BlockSpec((1, HEAD_DIM), lambda b, h: (0, 0))
    blk = pl.BlockSpec((1, seq, PAIR_DIM), lambda b, h: (b, 0, h))
    out = pl.pallas_call(
        functools.partial(_attn_prompt_kernel, seq=seq, lam_init=lam_init),
        grid=(batch, N_HEADS),
        in_specs=[vec, vec, vec, vec, pl.BlockSpec((1, PAIR_DIM), lambda b, h: (0, 0)), blk, blk, blk],
        out_specs=blk,
        out_shape=jax.ShapeDtypeStruct((batch, seq, D_ATTN), BF16),
        scratch_shapes=[pltpu.VMEM((seq_pad, PAIR_DIM), BF16), pltpu.VMEM((seq_pad, PAIR_DIM), BF16)],
        compiler_params=_params("parallel", "parallel"),
        name="attn_prompt",
    )(*lams, sub_gain, q3, k3, v3)
    return out.reshape(batch * seq, D_ATTN)


def _attn_sample_kernel(pt_ref, lq1, lk1, lq2, lk2, sg_ref, q_ref, kn_ref, vn_ref, *rest,
                        n_q, page, lam_init):
    del pt_ref
    npg = PAGES_PER_STEP
    k_refs, v_refs = rest[:npg], rest[npg:2 * npg]
    o_ref, qbd_ref, m_ref, l_ref, acc_ref = rest[2 * npg:]
    s_idx = pl.program_id(1)
    rows = 2 * N_HEADS * n_q
    grp = 2 * N_HEADS

    @pl.when(s_idx == 0)
    def _():
        q = q_ref[0]
        rep = jnp.concatenate([jnp.broadcast_to(q[i:i + 1, :], (grp, D_ATTN)) for i in range(n_q)], axis=0)
        r = lax.broadcasted_iota(jnp.int32, (rows, D_ATTN), 0)
        col = lax.broadcasted_iota(jnp.int32, (rows, D_ATTN), 1)
        head16 = 2 * (r % N_HEADS) + (r // N_HEADS) % 2
        qbd_ref[...] = jnp.where(col // HEAD_DIM == head16, rep, jnp.zeros_like(rep))
        m_ref[...] = jnp.full(m_ref.shape, NEG_INF, F32)
        l_ref[...] = jnp.zeros(l_ref.shape, F32)
        acc_ref[...] = jnp.zeros(acc_ref.shape, F32)

    qbd = qbd_ref[...]

    def update(s, v_pages):
        m = m_ref[...]
        m_new = jnp.maximum(m, jnp.max(s, axis=-1, keepdims=True))
        alpha = jnp.exp(m - m_new)
        p = jnp.exp(s - m_new)
        l_ref[...] = alpha * l_ref[...] + jnp.sum(p, axis=-1, keepdims=True)
        acc = alpha * acc_ref[...]
        for i, vp in enumerate(v_pages):
            acc = acc + jnp.dot(p[:, i * page:(i + 1) * page].astype(BF16), vp, preferred_element_type=F32)
        acc_ref[...] = acc
        m_ref[...] = m_new

    def scores(kp):
        return lax.dot_general(qbd, kp, (((1,), (1,)), ((), ())), preferred_element_type=F32)

    s = jnp.concatenate([scores(kr[0, 0].astype(BF16)) for kr in k_refs], axis=1)
    update(s, [vr[0, 0].astype(BF16) for vr in v_refs])

    @pl.when(s_idx == pl.num_programs(1) - 1)
    def _():
        padk = jnp.zeros((page - kn_ref.shape[1], D_ATTN), BF16)
        kn = jnp.concatenate([kn_ref[0].astype(BF16), padk], axis=0)
        vn = jnp.concatenate([vn_ref[0].astype(BF16), padk], axis=0)
        qi = lax.broadcasted_iota(jnp.int32, (rows, page), 0) // grp
        kj = lax.broadcasted_iota(jnp.int32, (rows, page), 1)
        update(jnp.where(kj <= qi, scores(kn), NEG_INF), [vn])

        r = lax.broadcasted_iota(jnp.int32, (rows, D_ATTN), 0)
        col = lax.broadcasted_iota(jnp.int32, (rows, D_ATTN), 1)
        kept = jnp.where(col // PAIR_DIM == r % N_HEADS, acc_ref[...], 0.0)
        f = kept[:, 0:PAIR_DIM]
        for c in range(1, N_HEADS):
            f = f + kept[:, c * PAIR_DIM:(c + 1) * PAIR_DIM]
        f = f / l_ref[...]
        lam = _diff_lambda(lq1, lk1, lq2, lk2, lam_init)
        for i in range(n_q):
            o = f[i * grp:i * grp + N_HEADS] - lam * f[i * grp + N_HEADS:(i + 1) * grp]
            o_ref[0, i] = _sub_ln(o, sg_ref[...], lam_init)


def _attn_sample(page_table, lams, sub_gain, q, k_new, v_new, cache_k, cache_v, layer, lam_init):
    db, n_pages = page_table.shape
    n_q = q.shape[0] // db
    page = cache_k.shape[2]
    npg = PAGES_PER_STEP
    rows = 2 * N_HEADS * n_q
    ck = cache_k.reshape(cache_k.shape[0], cache_k.shape[1], page, D_ATTN)
    cv = cache_v.reshape(cache_v.shape[0], cache_v.shape[1], page, D_ATTN)
    q3 = q.reshape(db, n_q, D_ATTN)
    pad_rows = 8 - n_q
    kn = jnp.pad(k_new.reshape(db, n_q, D_ATTN), ((0, 0), (0, pad_rows), (0, 0)))
    vn = jnp.pad(v_new.reshape(db, n_q, D_ATTN), ((0, 0), (0, pad_rows), (0, 0)))

    vec = pl.BlockSpec((1, HEAD_DIM), lambda b, s, pt: (0, 0))

    def page_spec(p):
        return pl.BlockSpec((1, 1, page, D_ATTN), lambda b, s, pt: (layer, pt[b, s * npg + p], 0, 0))

    tok = pl.BlockSpec((1, 8, D_ATTN), lambda b, s, pt: (b, 0, 0))
    grid_spec = pltpu.PrefetchScalarGridSpec(
        num_scalar_prefetch=1,
        grid=(db, n_pages // npg),
        in_specs=[vec, vec, vec, vec, pl.BlockSpec((1, PAIR_DIM), lambda b, s, pt: (0, 0)),
                  pl.BlockSpec((1, n_q, D_ATTN), lambda b, s, pt: (b, 0, 0)), tok, tok]
                 + [page_spec(p) for p in range(npg)] + [page_spec(p) for p in range(npg)],
        out_specs=pl.BlockSpec((1, n_q, N_HEADS, PAIR_DIM), lambda b, s, pt: (b, 0, 0, 0)),
        scratch_shapes=[pltpu.VMEM((rows, D_ATTN), BF16), pltpu.VMEM((rows, 1), F32),
                        pltpu.VMEM((rows, 1), F32), pltpu.VMEM((rows, D_ATTN), F32)],
    )
    out = pl.pallas_call(
        functools.partial(_attn_sample_kernel, n_q=n_q, page=page, lam_init=lam_init),
        grid_spec=grid_spec,
        out_shape=jax.ShapeDtypeStruct((db, n_q, N_HEADS, PAIR_DIM), F32),
        compiler_params=_params("parallel", "arbitrary"),
        name="attn_sample",
    )(page_table, *lams, sub_gain, q3, kn, vn, *([ck] * npg), *([cv] * npg))
    return out.reshape(db * n_q, D_ATTN).astype(BF16)


def _ln_silu(cf, g, b):
    mu = jnp.mean(cf, axis=-1, keepdims=True)
    d = cf - mu
    var = jnp.mean(d * d, axis=-1, keepdims=True)
    cn = d * lax.rsqrt(var + EPS) * g + b
    return cn * jax.nn.sigmoid(cn)


def _conv_prompt_kernel(h_ref, w_ref, b_ref, g_ref, bb_ref, o_ref, ext_ref, cv_ref, *, seq):
    lanes = 128
    ext_ref[pl.ds(0, CONV_HALO), :] = jnp.zeros((CONV_HALO, D_CONV), F32)
    ext_ref[pl.ds(CONV_HALO, seq), :] = h_ref[0]
    shift0 = CONV_HALO - (CONV_W - 1)
    win = CONV_ROWS + CONV_HALO

    def tile(it, carry):
        t0 = pl.multiple_of(it * CONV_ROWS, 8)
        for c in range(D_CONV // lanes):
            cols = pl.ds(c * lanes, lanes)
            w = ext_ref[pl.ds(t0, win), cols]
            acc = jnp.zeros((CONV_ROWS, lanes), F32)
            for r in range(8):
                n_a = (CONV_W - 1 + shift0 - r) // 8 + 1
                e = w[r:r + 8 * (n_a - 1) + CONV_ROWS]
                for a in range(n_a):
                    tap = 8 * a + r - shift0
                    if tap < 0:
                        continue
                    acc = acc + e[8 * a:8 * a + CONV_ROWS] * w_ref[pl.ds(tap, 1), cols]
            cv_ref[:, cols] = acc
        cf = cv_ref[...] + b_ref[...]
        o_ref[0, pl.ds(t0, CONV_ROWS), :] = _ln_silu(cf, g_ref[...], bb_ref[...]).astype(BF16)
        return carry

    lax.fori_loop(0, seq // CONV_ROWS, tile, 0)


def _conv_prompt(h, dw_w, dw_b, ln_g, ln_b, batch, seq):
    h3 = h.reshape(batch, seq, D_CONV)
    row = pl.BlockSpec((1, D_CONV), lambda b: (0, 0))
    out = pl.pallas_call(
        functools.partial(_conv_prompt_kernel, seq=seq),
        grid=(batch,),
        in_specs=[pl.BlockSpec((1, seq, D_CONV), lambda b: (b, 0, 0)),
                  pl.BlockSpec((CONV_W, D_CONV), lambda b: (0, 0)), row, row, row],
        out_specs=pl.BlockSpec((1, seq, D_CONV), lambda b: (b, 0, 0)),
        out_shape=jax.ShapeDtypeStruct((batch, seq, D_CONV), BF16),
        scratch_shapes=[pltpu.VMEM((CONV_HALO + seq, D_CONV), F32), pltpu.VMEM((CONV_ROWS, D_CONV), F32)],
        compiler_params=_params("parallel"),
        name="conv_prompt",
    )(h3, dw_w, dw_b, ln_g, ln_b)
    return out.reshape(batch * seq, D_CONV)


def _conv_sample_kernel(ext_ref, w_ref, b_ref, g_ref, bb_ref, o_ref, *, n_q):
    w = w_ref[...]
    for t in range(n_q):
        cf = jnp.sum(ext_ref[0, pl.ds(t, CONV_W), :] * w, axis=0, keepdims=True) + b_ref[...]
        o_ref[0, pl.ds(t, 1), :] = _ln_silu(cf, g_ref[...], bb_ref[...])


def _conv_sample(ext, dw_w, dw_b, ln_g, ln_b, n_q):
    db, rows, _ = ext.shape
    row = pl.BlockSpec((1, D_CONV), lambda b: (0, 0))
    out = pl.pallas_call(
        functools.partial(_conv_sample_kernel, n_q=n_q),
        grid=(db,),
        in_specs=[pl.BlockSpec((1, rows, D_CONV), lambda b: (b, 0, 0)),
                  pl.BlockSpec((CONV_W, D_CONV), lambda b: (0, 0)), row, row, row],
        out_specs=pl.BlockSpec((1, n_q, D_CONV), lambda b: (b, 0, 0)),
        out_shape=jax.ShapeDtypeStruct((db, n_q, D_CONV), F32),
        compiler_params=_params("parallel"),
        name="conv_sample",
    )(ext, dw_w, dw_b, ln_g, ln_b)
    return out.reshape(db * n_q, D_CONV).astype(BF16)


def _out_proj_kernel(x_ref, oa_ref, c_ref, wa_ref, wc_ref, o_ref):
    o_ref[...] = (x_ref[...]
                  + jnp.dot(oa_ref[...], wa_ref[...], preferred_element_type=F32)
                  + jnp.dot(c_ref[...], wc_ref[...], preferred_element_type=F32))


def _out_proj(x, oa, c, w_bf16, tm):
    m = x.shape[0]
    tn = D_MODEL // 2
    return pl.pallas_call(
        _out_proj_kernel,
        grid=(m // tm, D_MODEL // tn),
        in_specs=[pl.BlockSpec((tm, tn), lambda i, j: (i, j)),
                  pl.BlockSpec((tm, D_ATTN), lambda i, j: (i, 0)),
                  pl.BlockSpec((tm, D_CONV), lambda i, j: (i, 0)),
                  pl.BlockSpec((D_ATTN, tn), lambda i, j: (0, j)),
                  pl.BlockSpec((D_CONV, tn), lambda i, j: (D_ATTN // D_CONV, j))],
        out_specs=pl.BlockSpec((tm, tn), lambda i, j: (i, j)),
        out_shape=jax.ShapeDtypeStruct((m, D_MODEL), F32),
        compiler_params=_params("parallel", "arbitrary"),
        name="out_proj",
    )(x, oa, c, w_bf16, w_bf16)


def _ffn_kernel(x_ref, g_ref, wu_ref, wd_ref, o_ref, xn_ref):
    @pl.when(pl.program_id(1) == 0)
    def _():
        x = x_ref[...]
        xn_ref[...] = _rms_rows(x, g_ref[...]).astype(BF16)
        o_ref[...] = x

    h = jnp.dot(xn_ref[...], wu_ref[...], preferred_element_type=F32)
    a = jnp.square(jnp.maximum(h, 0.0)).astype(BF16)
    o_ref[...] += jnp.dot(a, wd_ref[...], preferred_element_type=F32)


def _ffn(x, gain, wu_bf16, wd_bf16, tm, tf):
    m = x.shape[0]
    return pl.pallas_call(
        _ffn_kernel,
        grid=(m // tm, D_FF // tf),
        in_specs=[pl.BlockSpec((tm, D_MODEL), lambda i, j: (i, 0)),
                  pl.BlockSpec((1, D_MODEL), lambda i, j: (0, 0)),
                  pl.BlockSpec((D_MODEL, tf), lambda i, j: (0, j)),
                  pl.BlockSpec((tf, D_MODEL), lambda i, j: (j, 0))],
        out_specs=pl.BlockSpec((tm, D_MODEL), lambda i, j: (i, 0)),
        out_shape=jax.ShapeDtypeStruct((m, D_MODEL), F32),
        scratch_shapes=[pltpu.VMEM((tm, D_MODEL), BF16)],
        compiler_params=_params("parallel", "arbitrary"),
        name="ffn",
    )(x, gain, wu_bf16, wd_bf16)


def _row_tile(m):
    for tm in (1032, 1024, 688, 512, 344, 256, 128, 64, 32, 16, 8):
        if m % tm == 0:
            return tm
    raise ValueError(f"unsupported row count {m}")


def kernel(x_prompt, x_sample, cache_k, cache_v, state_conv, page_table, meta_tokens, attn_norm, w_in,
           q_norm, k_norm, lambda_q1, lambda_k1, lambda_q2, lambda_k2, attn_subln, dw_w, dw_b,
           conv_ln_g, conv_ln_b, w_out, mlp_norm, w_up, w_down):
    depth = w_in.shape[0]
    b, seq_real, _ = x_prompt.shape
    seq = N_META + seq_real
    db, ds, _ = x_sample.shape

    w_in_b, w_out_b = w_in.astype(BF16), w_out.astype(BF16)
    w_up_b, w_down_b = w_up.astype(BF16), w_down.astype(BF16)
    heads_per_chunk = IN_CHUNK // HEAD_DIM
    blk = jnp.arange(IN_CHUNK) // HEAD_DIM
    gm = jnp.where(blk[:, None] == blk[None, :], 1.0 / HEAD_DIM, 0.0).astype(BF16)

    meta = jnp.broadcast_to(meta_tokens.astype(x_prompt.dtype)[None], (b, N_META, D_MODEL))
    xp = jnp.concatenate([meta, x_prompt], axis=1).reshape(b * seq, D_MODEL)
    xs = x_sample.reshape(db * ds, D_MODEL)
    tm_p, tm_s = _row_tile(b * seq), _row_tile(db * ds)

    kp, vp, cp, ks_, vs_, cs_ = [], [], [], [], [], []
    for l in range(depth):
        lam0 = lambda_init(l)
        lams = [a[l][None, :] for a in (lambda_q1, lambda_k1, lambda_q2, lambda_k2)]
        sub_gain = attn_subln[l][None, :]
        an, mn = attn_norm[l][None, :], mlp_norm[l][None, :]
        qg = jnp.tile(q_norm[l], heads_per_chunk)[None, :]
        kg = jnp.tile(k_norm[l], heads_per_chunk)[None, :]
        cw, cb = dw_w[l], dw_b[l][None, :]
        lg, lb = conv_ln_g[l][None, :], conv_ln_b[l][None, :]

        q, k, v, h = _in_proj(xp, an, w_in_b[l], gm, qg, kg, tm_p)
        oa = _attn_prompt(lams, sub_gain, q, k, v, b, seq, lam0)
        c = _conv_prompt(h, cw, cb, lg, lb, b, seq)
        x1 = _out_proj(xp, oa, c, w_out_b[l], tm_p)
        xp = _ffn(x1, mn, w_up_b[l], w_down_b[l], tm_p, 512)
        kp.append(k.reshape(b, seq, 2 * N_HEADS, HEAD_DIM))
        vp.append(v.reshape(b, seq, N_HEADS, PAIR_DIM))
        cp.append(h.reshape(b, seq, D_CONV)[:, seq - (CONV_W - 1):])

        q, k, v, h = _in_proj(xs, an, w_in_b[l], gm, qg, kg, tm_s)
        oa = _attn_sample(page_table, lams, sub_gain, q, k, v, cache_k, cache_v, l, lam0)
        ext = jnp.concatenate([state_conv[l], h.reshape(db, ds, D_CONV)], axis=1)
        c = _conv_sample(ext, cw, cb, lg, lb, ds)
        x1 = _out_proj(xs, oa, c, w_out_b[l], tm_s)
        xs = _ffn(x1, mn, w_up_b[l], w_down_b[l], tm_s, 512)
        ks_.append(k.reshape(db, ds, 2 * N_HEADS, HEAD_DIM))
        vs_.append(v.reshape(db, ds, N_HEADS, PAIR_DIM))
        cs_.append(ext[:, ds:])

    y_prompt = xp.reshape(b, seq, D_MODEL)[:, N_META:]
    y_sample = xs.reshape(db, ds, D_MODEL)
    return (y_prompt, y_sample, jnp.stack(kp), jnp.stack(vp), jnp.stack(cp),
            jnp.stack(ks_), jnp.stack(vs_), jnp.stack(cs_))
```

```python
import functools
import math

import jax
import jax.numpy as jnp
from jax import lax
from jax.experimental import pallas as pl
from jax.experimental.pallas import tpu as pltpu

F32 = jnp.float32
BF16 = jnp.bfloat16

D_MODEL = 2048
N_META = 16
D_ATTN = D_MODEL // 2
N_HEADS = 8
HEAD_DIM = D_ATTN // (2 * N_HEADS)
PAIR_DIM = 2 * HEAD_DIM
D_CONV = D_MODEL - D_ATTN
CONV_W = 31
D_FF = 4 * D_MODEL
D_IN = 3 * D_ATTN + 2 * D_CONV
EPS = 1e-6
NEG_INF = -1e30
QK_SCALE = HEAD_DIM ** -0.5

VMEM_LIMIT_BYTES = 56 * 1024 * 1024

IN_CHUNK = 512
ATTN_TILE = 256
CONV_ROWS = 48
CONV_HALO = 32
PAGES_PER_STEP = 8


def lambda_init(l):
    return 0.8 - 0.6 * math.exp(-0.3 * l)


def _params(*semantics):
    return pltpu.CompilerParams(dimension_semantics=semantics, vmem_limit_bytes=VMEM_LIMIT_BYTES)


def _rms_rows(x, gain):
    ms = jnp.mean(x * x, axis=-1, keepdims=True)
    return x * lax.rsqrt(ms + EPS) * gain


def _diff_lambda(lq1, lk1, lq2, lk2, lam_init):
    e1 = jnp.exp(jnp.sum(lq1[...] * lk1[...], axis=-1, keepdims=True))
    e2 = jnp.exp(jnp.sum(lq2[...] * lk2[...], axis=-1, keepdims=True))
    return e1 - e2 + lam_init


def _in_proj_kernel(x_ref, g_ref, w_ref, wg_ref, gm_ref, qg_ref, kg_ref,
                    q_ref, k_ref, v_ref, h_ref, xn_ref):
    j = pl.program_id(1)
    n_q = D_ATTN // IN_CHUNK

    @pl.when(j == 0)
    def _():
        xn_ref[...] = _rms_rows(x_ref[...], g_ref[...]).astype(BF16)

    xn = xn_ref[...]
    z = jnp.dot(xn, w_ref[...], preferred_element_type=F32)

    def head_norm(gain):
        ms = jnp.dot((z * z).astype(BF16), gm_ref[...], preferred_element_type=F32)
        return z * lax.rsqrt(ms + EPS) * gain

    @pl.when(j < n_q)
    def _():
        q_ref[...] = (head_norm(qg_ref[...]) * QK_SCALE).astype(BF16)

    @pl.when((j >= n_q) & (j < 2 * n_q))
    def _():
        k_ref[...] = head_norm(kg_ref[...])

    @pl.when((j >= 2 * n_q) & (j < 3 * n_q))
    def _():
        v_ref[...] = z

    @pl.when(j >= 3 * n_q)
    def _():
        gate = jnp.dot(xn, wg_ref[...], preferred_element_type=F32)
        h_ref[...] = z * jax.nn.sigmoid(gate)


def _in_proj(x, gain, w_bf16, gm, qg, kg, tm):
    m = x.shape[0]
    n_q = D_ATTN // IN_CHUNK
    n_steps = 4 * n_q
    gate0 = (3 * D_ATTN + D_CONV) // IN_CHUNK

    def region(r):
        return lambda i, j: (i, jnp.clip(j - r * n_q, 0, n_q - 1))

    return pl.pallas_call(
        _in_proj_kernel,
        grid=(m // tm, n_steps),
        in_specs=[
            pl.BlockSpec((tm, D_MODEL), lambda i, j: (i, 0)),
            pl.BlockSpec((1, D_MODEL), lambda i, j: (0, 0)),
            pl.BlockSpec((D_MODEL, IN_CHUNK), lambda i, j: (0, j)),
            pl.BlockSpec((D_MODEL, IN_CHUNK), lambda i, j: (0, gate0 + jnp.maximum(j - 3 * n_q, 0))),
            pl.BlockSpec((IN_CHUNK, IN_CHUNK), lambda i, j: (0, 0)),
            pl.BlockSpec((1, IN_CHUNK), lambda i, j: (0, 0)),
            pl.BlockSpec((1, IN_CHUNK), lambda i, j: (0, 0)),
        ],
        out_specs=[
            pl.BlockSpec((tm, IN_CHUNK), region(0)),
            pl.BlockSpec((tm, IN_CHUNK), region(1)),
            pl.BlockSpec((tm, IN_CHUNK), region(2)),
            pl.BlockSpec((tm, IN_CHUNK), region(3)),
        ],
        out_shape=[
            jax.ShapeDtypeStruct((m, D_ATTN), BF16),
            jax.ShapeDtypeStruct((m, D_ATTN), F32),
            jax.ShapeDtypeStruct((m, D_ATTN), F32),
            jax.ShapeDtypeStruct((m, D_CONV), F32),
        ],
        scratch_shapes=[pltpu.VMEM((tm, D_MODEL), BF16)],
        compiler_params=_params("parallel", "arbitrary"),
        name="in_proj",
    )(x, gain, w_bf16, w_bf16, gm, qg, kg)


def _softmax_step(qq, kc, vc, carry, mask):
    m, l, acc = carry
    s = lax.dot_general(qq, kc, (((1,), (1,)), ((), ())), preferred_element_type=F32)
    if mask is not None:
        s = jnp.where(mask, s, NEG_INF)
    m_new = jnp.maximum(m, jnp.max(s, axis=-1, keepdims=True))
    alpha = jnp.exp(m - m_new)
    p = jnp.exp(s - m_new)
    l = alpha * l + jnp.sum(p, axis=-1, keepdims=True)
    acc = alpha * acc + jnp.dot(p.astype(BF16), vc, preferred_element_type=F32)
    return m_new, l, acc


def _sub_ln(o, gain, lam_init):
    ms = jnp.mean(o * o, axis=-1, keepdims=True)
    return o * lax.rsqrt(ms + EPS) * gain * (1.0 - lam_init)


def _attn_prompt_kernel(lq1, lk1, lq2, lk2, sg_ref, q_ref, k_ref, v_ref, o_ref,
                        kb_ref, vb_ref, *, seq, lam_init):
    lam = _diff_lambda(lq1, lk1, lq2, lk2, lam_init)
    pad = kb_ref.shape[0] - seq
    kb_ref[pl.ds(0, seq), :] = k_ref[0].astype(BF16)
    vb_ref[pl.ds(0, seq), :] = v_ref[0].astype(BF16)
    kb_ref[pl.ds(seq, pad), :] = jnp.zeros((pad, PAIR_DIM), BF16)
    vb_ref[pl.ds(seq, pad), :] = jnp.zeros((pad, PAIR_DIM), BF16)

    first = lax.broadcasted_iota(jnp.int32, (1, PAIR_DIM), 1) < HEAD_DIM
    n_tiles = pl.cdiv(seq, ATTN_TILE)
    for i in range(n_tiles):
        r0 = i * ATTN_TILE
        t = min(ATTN_TILE, seq - r0)
        qt = q_ref[0, pl.ds(r0, t), :]
        zero = jnp.zeros_like(qt)
        qq = jnp.concatenate([jnp.where(first, qt, zero), jnp.where(first, zero, qt)], axis=0)
        carry = (jnp.full((2 * t, 1), NEG_INF, F32), jnp.zeros((2 * t, 1), F32),
                 jnp.zeros((2 * t, PAIR_DIM), F32))

        def body(c, carry, qq=qq):
            c0 = pl.multiple_of(c * ATTN_TILE, ATTN_TILE)
            return _softmax_step(qq, kb_ref[pl.ds(c0, ATTN_TILE), :],
                                 vb_ref[pl.ds(c0, ATTN_TILE), :], carry, None)

        if i > 0:
            carry = lax.fori_loop(0, i, body, carry)
        qpos = lax.broadcasted_iota(jnp.int32, (2 * t, ATTN_TILE), 0) % t
        kpos = lax.broadcasted_iota(jnp.int32, (2 * t, ATTN_TILE), 1)
        carry = _softmax_step(qq, kb_ref[pl.ds(r0, ATTN_TILE), :], vb_ref[pl.ds(r0, ATTN_TILE), :],
                              carry, kpos <= qpos)
        _, l, acc = carry
        o = acc[:t] / l[:t] - lam * (acc[t:] / l[t:])
        o_ref[0, pl.ds(r0, t), :] = _sub_ln(o, sg_ref[...], lam_init).astype(BF16)


def _attn_prompt(lams, sub_gain, q, k, v, batch, seq, lam_init):
    q3 = q.reshape(batch, seq, D_ATTN)
    k3 = k.reshape(batch, seq, D_ATTN)
    v3 = v.reshape(batch, seq, D_ATTN)
    seq_pad = pl.cdiv(seq, ATTN_TILE) * ATTN_TILE
    vec = pl.BlockSpec((1, HEAD_DIM), lambda b, h: (0, 0))
    blk = pl.BlockSpec((1, seq, PAIR_DIM), lambda b, h: (b, 0, h))
    out = pl.pallas_call(
        functools.partial(_attn_prompt_kernel, seq=seq, lam_init=lam_init),
        grid=(batch, N_HEADS),
        in_specs=[vec, vec, vec, vec, pl.BlockSpec((1, PAIR_DIM), lambda b, h: (0, 0)), blk, blk, blk],
        out_specs=blk,
        out_shape=jax.ShapeDtypeStruct((batch, seq, D_ATTN), BF16),
        scratch_shapes=[pltpu.VMEM((seq_pad, PAIR_DIM), BF16), pltpu.VMEM((seq_pad, PAIR_DIM), BF16)],
        compiler_params=_params("parallel", "parallel"),
        name="attn_prompt",
    )(*lams, sub_gain, q3, k3, v3)
    return out.reshape(batch * seq, D_ATTN)


SAMPLE_ROWS = 8


def _attn_sample_kernel(pt_ref, lq1, lk1, lq2, lk2, sg_ref, qa_ref, qb_ref, kn_ref, vn_ref, *rest,
                        n_q, page, lam_init):
    del pt_ref
    npg = PAGES_PER_STEP
    k_refs, v_refs = rest[:npg], rest[npg:2 * npg]
    o_ref, m_ref, l_ref, acc_ref = rest[2 * npg:]
    s_idx = pl.program_id(1)
    rows = N_HEADS * SAMPLE_ROWS
    half = SAMPLE_ROWS // 2
    nt = (((1,), (1,)), ((), ()))

    @pl.when(s_idx == 0)
    def _():
        m_ref[...] = jnp.full(m_ref.shape, NEG_INF, F32)
        l_ref[...] = jnp.zeros(l_ref.shape, F32)
        acc_ref[...] = jnp.zeros(acc_ref.shape, F32)

    qa = [qa_ref[0, h].astype(BF16) for h in range(N_HEADS)]
    qb = [qb_ref[0, h].astype(BF16) for h in range(N_HEADS)]

    def scores(k_head):
        blocks = []
        for h in range(N_HEADS):
            s1 = lax.dot_general(qa[h], k_head(2 * h).astype(BF16), nt, preferred_element_type=F32)
            s2 = lax.dot_general(qb[h], k_head(2 * h + 1).astype(BF16), nt, preferred_element_type=F32)
            blocks.append(s1 + s2)
        return jnp.concatenate(blocks, axis=0)

    def update(s, v_heads):
        m = m_ref[...]
        m_new = jnp.maximum(m, jnp.max(s, axis=-1, keepdims=True))
        alpha = jnp.exp(m - m_new)
        p = jnp.exp(s - m_new)
        l_ref[...] = alpha * l_ref[...] + jnp.sum(p, axis=-1, keepdims=True)
        pv = []
        for h in range(N_HEADS):
            ph = p[h * SAMPLE_ROWS:(h + 1) * SAMPLE_ROWS]
            acc = None
            for i, v_head in enumerate(v_heads):
                d = jnp.dot(ph[:, i * page:(i + 1) * page].astype(BF16), v_head(h).astype(BF16),
                            preferred_element_type=F32)
                acc = d if acc is None else acc + d
            pv.append(acc)
        acc_ref[...] = alpha * acc_ref[...] + jnp.concatenate(pv, axis=0)
        m_ref[...] = m_new

    def k_slice(ref, lead):
        return lambda h16: ref[(*lead, pl.ds(h16, page, stride=2 * N_HEADS), slice(None))]

    def v_slice(ref, lead):
        return lambda h: ref[(*lead, pl.ds(h, page, stride=N_HEADS), slice(None))]

    s = jnp.concatenate([scores(k_slice(kr, (0, 0))) for kr in k_refs], axis=1)
    update(s, [v_slice(vr, (0, 0)) for vr in v_refs])

    @pl.when(s_idx == pl.num_programs(1) - 1)
    def _():
        qi = lax.broadcasted_iota(jnp.int32, (rows, page), 0) % half
        kj = lax.broadcasted_iota(jnp.int32, (rows, page), 1)
        s_new = scores(k_slice(kn_ref, (0,)))
        update(jnp.where(kj <= qi, s_new, NEG_INF), [v_slice(vn_ref, (0,))])

        f = acc_ref[...] / l_ref[...]
        lam = _diff_lambda(lq1, lk1, lq2, lk2, lam_init)
        for h in range(N_HEADS):
            r0 = h * SAMPLE_ROWS
            o = f[r0:r0 + SAMPLE_ROWS] - lam * pltpu.roll(f[r0:r0 + SAMPLE_ROWS], half, 0)
            o_ref[0, h] = _sub_ln(o, sg_ref[...], lam_init)


def _attn_sample(page_table, lams, sub_gain, q, k_new, v_new, cache_k, cache_v, layer, lam_init):
    db, n_pages = page_table.shape
    n_q = q.shape[0] // db
    page = cache_k.shape[2]
    npg = PAGES_PER_STEP
    rows = N_HEADS * SAMPLE_ROWS
    half = SAMPLE_ROWS // 2
    assert n_q <= half

    q5 = q.astype(F32).reshape(db, n_q, N_HEADS, 2, HEAD_DIM).transpose(0, 2, 3, 1, 4)
    q5 = jnp.pad(q5, ((0, 0), (0, 0), (0, 0), (0, half - n_q), (0, 0)))
    zeros = jnp.zeros_like(q5[:, :, 0])
    qa = jnp.concatenate([q5[:, :, 0], zeros], axis=2)
    qb = jnp.concatenate([zeros, q5[:, :, 1]], axis=2)
    k_rows, v_rows = page * 2 * N_HEADS, page * N_HEADS
    kn = jnp.pad(k_new.reshape(db, n_q * 2 * N_HEADS, HEAD_DIM), ((0, 0), (0, k_rows - n_q * 2 * N_HEADS), (0, 0)))
    vn = jnp.pad(v_new.reshape(db, n_q * N_HEADS, PAIR_DIM), ((0, 0), (0, v_rows - n_q * N_HEADS), (0, 0)))
    ck = cache_k.reshape(cache_k.shape[0], cache_k.shape[1], k_rows, HEAD_DIM)
    cv = cache_v.reshape(cache_v.shape[0], cache_v.shape[1], v_rows, PAIR_DIM)

    vec = pl.BlockSpec((1, HEAD_DIM), lambda b, s, pt: (0, 0))
    qspec = pl.BlockSpec((1, N_HEADS, SAMPLE_ROWS, HEAD_DIM), lambda b, s, pt: (b, 0, 0, 0))

    def page_spec(p, n_rows, dim):
        return pl.BlockSpec((1, 1, n_rows, dim), lambda b, s, pt: (layer, pt[b, s * npg + p], 0, 0))

    grid_spec = pltpu.PrefetchScalarGridSpec(
        num_scalar_prefetch=1,
        grid=(db, n_pages // npg),
        in_specs=[vec, vec, vec, vec, pl.BlockSpec((1, PAIR_DIM), lambda b, s, pt: (0, 0)), qspec, qspec,
                  pl.BlockSpec((1, k_rows, HEAD_DIM), lambda b, s, pt: (b, 0, 0)),
                  pl.BlockSpec((1, v_rows, PAIR_DIM), lambda b, s, pt: (b, 0, 0))]
                 + [page_spec(p, k_rows, HEAD_DIM) for p in range(npg)]
                 + [page_spec(p, v_rows, PAIR_DIM) for p in range(npg)],
        out_specs=pl.BlockSpec((1, N_HEADS, SAMPLE_ROWS, PAIR_DIM), lambda b, s, pt: (b, 0, 0, 0)),
        scratch_shapes=[pltpu.VMEM((rows, 1), F32), pltpu.VMEM((rows, 1), F32),
                        pltpu.VMEM((rows, PAIR_DIM), F32)],
    )
    out = pl.pallas_call(
        functools.partial(_attn_sample_kernel, n_q=n_q, page=page, lam_init=lam_init),
        grid_spec=grid_spec,
        out_shape=jax.ShapeDtypeStruct((db, N_HEADS, SAMPLE_ROWS, PAIR_DIM), F32),
        compiler_params=_params("parallel", "arbitrary"),
        name="attn_sample",
    )(page_table, *lams, sub_gain, qa, qb, kn, vn, *([ck] * npg), *([cv] * npg))
    return out[:, :, :n_q].transpose(0, 2, 1, 3).reshape(db * n_q, D_ATTN).astype(BF16)


def _ln_silu(cf, g, b):
    mu = jnp.mean(cf, axis=-1, keepdims=True)
    d = cf - mu
    var = jnp.mean(d * d, axis=-1, keepdims=True)
    cn = d * lax.rsqrt(var + EPS) * g + b
    return cn * jax.nn.sigmoid(cn)


def _conv_prompt_kernel(h_ref, w_ref, b_ref, g_ref, bb_ref, o_ref, ext_ref, cv_ref, *, seq):
    lanes = 128
    ext_ref[pl.ds(0, CONV_HALO), :] = jnp.zeros((CONV_HALO, D_CONV), F32)
    ext_ref[pl.ds(CONV_HALO, seq), :] = h_ref[0]
    shift0 = CONV_HALO - (CONV_W - 1)
    win = CONV_ROWS + CONV_HALO

    def tile(it, carry):
        t0 = pl.multiple_of(it * CONV_ROWS, 8)
        for c in range(D_CONV // lanes):
            cols = pl.ds(c * lanes, lanes)
            w = ext_ref[pl.ds(t0, win), cols]
            acc = jnp.zeros((CONV_ROWS, lanes), F32)
            for r in range(8):
                n_a = (CONV_W - 1 + shift0 - r) // 8 + 1
                e = w[r:r + 8 * (n_a - 1) + CONV_ROWS]
                for a in range(n_a):
                    tap = 8 * a + r - shift0
                    if tap < 0:
                        continue
                    acc = acc + e[8 * a:8 * a + CONV_ROWS] * w_ref[pl.ds(tap, 1), cols]
            cv_ref[:, cols] = acc
        cf = cv_ref[...] + b_ref[...]
        o_ref[0, pl.ds(t0, CONV_ROWS), :] = _ln_silu(cf, g_ref[...], bb_ref[...]).astype(BF16)
        return carry

    lax.fori_loop(0, seq // CONV_ROWS, tile, 0)


def _conv_prompt(h, dw_w, dw_b, ln_g, ln_b, batch, seq):
    h3 = h.reshape(batch, seq, D_CONV)
    row = pl.BlockSpec((1, D_CONV), lambda b: (0, 0))
    out = pl.pallas_call(
        functools.partial(_conv_prompt_kernel, seq=seq),
        grid=(batch,),
        in_specs=[pl.BlockSpec((1, seq, D_CONV), lambda b: (b, 0, 0)),
                  pl.BlockSpec((CONV_W, D_CONV), lambda b: (0, 0)), row, row, row],
        out_specs=pl.BlockSpec((1, seq, D_CONV), lambda b: (b, 0, 0)),
        out_shape=jax.ShapeDtypeStruct((batch, seq, D_CONV), BF16),
        scratch_shapes=[pltpu.VMEM((CONV_HALO + seq, D_CONV), F32), pltpu.VMEM((CONV_ROWS, D_CONV), F32)],
        compiler_params=_params("parallel"),
        name="conv_prompt",
    )(h3, dw_w, dw_b, ln_g, ln_b)
    return out.reshape(batch * seq, D_CONV)


def _conv_sample_kernel(ext_ref, w_ref, b_ref, g_ref, bb_ref, o_ref, *, n_q):
    w = w_ref[...]
    for t in range(n_q):
        cf = jnp.sum(ext_ref[0, pl.ds(t, CONV_W), :] * w, axis=0, keepdims=True) + b_ref[...]
        o_ref[0, pl.ds(t, 1), :] = _ln_silu(cf, g_ref[...], bb_ref[...])


def _conv_sample(ext, dw_w, dw_b, ln_g, ln_b, n_q):
    db, rows, _ = ext.shape
    row = pl.BlockSpec((1, D_CONV), lambda b: (0, 0))
    out = pl.pallas_call(
        functools.partial(_conv_sample_kernel, n_q=n_q),
        grid=(db,),
        in_specs=[pl.BlockSpec((1, rows, D_CONV), lambda b: (b, 0, 0)),
                  pl.BlockSpec((CONV_W, D_CONV), lambda b: (0, 0)), row, row, row],
        out_specs=pl.BlockSpec((1, n_q, D_CONV), lambda b: (b, 0, 0)),
        out_shape=jax.ShapeDtypeStruct((db, n_q, D_CONV), F32),
        compiler_params=_params("parallel"),
        name="conv_sample",
    )(ext, dw_w, dw_b, ln_g, ln_b)
    return out.reshape(db * n_q, D_CONV).astype(BF16)


def _out_proj_kernel(x_ref, oa_ref, c_ref, wa_ref, wc_ref, o_ref):
    o_ref[...] = (x_ref[...]
                  + jnp.dot(oa_ref[...], wa_ref[...], preferred_element_type=F32)
                  + jnp.dot(c_ref[...], wc_ref[...], preferred_element_type=F32))


def _out_proj(x, oa, c, w_bf16, tm):
    m = x.shape[0]
    tn = D_MODEL // 2
    return pl.pallas_call(
        _out_proj_kernel,
        grid=(m // tm, D_MODEL // tn),
        in_specs=[pl.BlockSpec((tm, tn), lambda i, j: (i, j)),
                  pl.BlockSpec((tm, D_ATTN), lambda i, j: (i, 0)),
                  pl.BlockSpec((tm, D_CONV), lambda i, j: (i, 0)),
                  pl.BlockSpec((D_ATTN, tn), lambda i, j: (0, j)),
                  pl.BlockSpec((D_CONV, tn), lambda i, j: (D_ATTN // D_CONV, j))],
        out_specs=pl.BlockSpec((tm, tn), lambda i, j: (i, j)),
        out_shape=jax.ShapeDtypeStruct((m, D_MODEL), F32),
        compiler_params=_params("parallel", "arbitrary"),
        name="out_proj",
    )(x, oa, c, w_bf16, w_bf16)


def _ffn_kernel(x_ref, g_ref, wu_ref, wd_ref, o_ref, xn_ref):
    @pl.when(pl.program_id(1) == 0)
    def _():
        x = x_ref[...]
        xn_ref[...] = _rms_rows(x, g_ref[...]).astype(BF16)
        o_ref[...] = x

    h = jnp.dot(xn_ref[...], wu_ref[...], preferred_element_type=F32)
    a = jnp.square(jnp.maximum(h, 0.0)).astype(BF16)
    o_ref[...] += jnp.dot(a, wd_ref[...], preferred_element_type=F32)


def _ffn(x, gain, wu_bf16, wd_bf16, tm, tf):
    m = x.shape[0]
    return pl.pallas_call(
        _ffn_kernel,
        grid=(m // tm, D_FF // tf),
        in_specs=[pl.BlockSpec((tm, D_MODEL), lambda i, j: (i, 0)),
                  pl.BlockSpec((1, D_MODEL), lambda i, j: (0, 0)),
                  pl.BlockSpec((D_MODEL, tf), lambda i, j: (0, j)),
                  pl.BlockSpec((tf, D_MODEL), lambda i, j: (j, 0))],
        out_specs=pl.BlockSpec((tm, D_MODEL), lambda i, j: (i, 0)),
        out_shape=jax.ShapeDtypeStruct((m, D_MODEL), F32),
        scratch_shapes=[pltpu.VMEM((tm, D_MODEL), BF16)],
        compiler_params=_params("parallel", "arbitrary"),
        name="ffn",
    )(x, gain, wu_bf16, wd_bf16)


def _row_tile(m):
    for tm in (1032, 1024, 688, 512, 344, 256, 128, 64, 32, 16, 8):
        if m % tm == 0:
            return tm
    raise ValueError(f"unsupported row count {m}")


def kernel(x_prompt, x_sample, cache_k, cache_v, state_conv, page_table, meta_tokens, attn_norm, w_in,
           q_norm, k_norm, lambda_q1, lambda_k1, lambda_q2, lambda_k2, attn_subln, dw_w, dw_b,
           conv_ln_g, conv_ln_b, w_out, mlp_norm, w_up, w_down):
    depth = w_in.shape[0]
    b, seq_real, _ = x_prompt.shape
    seq = N_META + seq_real
    db, ds, _ = x_sample.shape

    w_in_b, w_out_b = w_in.astype(BF16), w_out.astype(BF16)
    w_up_b, w_down_b = w_up.astype(BF16), w_down.astype(BF16)
    heads_per_chunk = IN_CHUNK // HEAD_DIM
    blk = jnp.arange(IN_CHUNK) // HEAD_DIM
    gm = jnp.where(blk[:, None] == blk[None, :], 1.0 / HEAD_DIM, 0.0).astype(BF16)

    meta = jnp.broadcast_to(meta_tokens.astype(x_prompt.dtype)[None], (b, N_META, D_MODEL))
    xp = jnp.concatenate([meta, x_prompt], axis=1).reshape(b * seq, D_MODEL)
    xs = x_sample.reshape(db * ds, D_MODEL)
    tm_p, tm_s = _row_tile(b * seq), _row_tile(db * ds)

    kp, vp, cp, ks_, vs_, cs_ = [], [], [], [], [], []
    for l in range(depth):
        lam0 = lambda_init(l)
        lams = [a[l][None, :] for a in (lambda_q1, lambda_k1, lambda_q2, lambda_k2)]
        sub_gain = attn_subln[l][None, :]
        an, mn = attn_norm[l][None, :], mlp_norm[l][None, :]
        qg = jnp.tile(q_norm[l], heads_per_chunk)[None, :]
        kg = jnp.tile(k_norm[l], heads_per_chunk)[None, :]
        cw, cb = dw_w[l], dw_b[l][None, :]
        lg, lb = conv_ln_g[l][None, :], conv_ln_b[l][None, :]

        q, k, v, h = _in_proj(xp, an, w_in_b[l], gm, qg, kg, tm_p)
        oa = _attn_prompt(lams, sub_gain, q, k, v, b, seq, lam0)
        c = _conv_prompt(h, cw, cb, lg, lb, b, seq)
        x1 = _out_proj(xp, oa, c, w_out_b[l], tm_p)
        xp = _ffn(x1, mn, w_up_b[l], w_down_b[l], tm_p, 512)
        kp.append(k.reshape(b, seq, 2 * N_HEADS, HEAD_DIM))
        vp.append(v.reshape(b, seq, N_HEADS, PAIR_DIM))
        cp.append(h.reshape(b, seq, D_CONV)[:, seq - (CONV_W - 1):])

        q, k, v, h = _in_proj(xs, an, w_in_b[l], gm, qg, kg, tm_s)
        oa = _attn_sample(page_table, lams, sub_gain, q, k, v, cache_k, cache_v, l, lam0)
        ext = jnp.concatenate([state_conv[l], h.reshape(db, ds, D_CONV)], axis=1)
        c = _conv_sample(ext, cw, cb, lg, lb, ds)
        x1 = _out_proj(xs, oa, c, w_out_b[l], tm_s)
        xs = _ffn(x1, mn, w_up_b[l], w_down_b[l], tm_s, 512)
        ks_.append(k.reshape(db, ds, 2 * N_HEADS, HEAD_DIM))
        vs_.append(v.reshape(db, ds, N_HEADS, PAIR_DIM))
        cs_.append(ext[:, ds:])

    y_prompt = xp.reshape(b, seq, D_MODEL)[:, N_META:]
    y_sample = xs.reshape(db, ds, D_MODEL)
    return (y_prompt, y_sample, jnp.stack(kp), jnp.stack(vp), jnp.stack(cp),
            jnp.stack(ks_), jnp.stack(vs_), jnp.stack(cs_))
```

```python
import functools
import math

import jax
import jax.numpy as jnp
from jax import lax
from jax.experimental import pallas as pl
from jax.experimental.pallas import tpu as pltpu

F32 = jnp.float32
BF16 = jnp.bfloat16

D_MODEL = 2048
N_META = 16
D_ATTN = D_MODEL // 2
N_HEADS = 8
HEAD_DIM = D_ATTN // (2 * N_HEADS)
PAIR_DIM = 2 * HEAD_DIM
D_CONV = D_MODEL - D_ATTN
CONV_W = 31
D_FF = 4 * D_MODEL
D_IN = 3 * D_ATTN + 2 * D_CONV
EPS = 1e-6
NEG_INF = -1e30
QK_SCALE = HEAD_DIM ** -0.5

VMEM_LIMIT_BYTES = 56 * 1024 * 1024

IN_CHUNK = 512
ATTN_TILE = 256
ATTN_TAIL_TILE = 128
CONV_ROWS = 48
CONV_HALO = 32
PAGES_PER_STEP = 8


def lambda_init(l):
    return 0.8 - 0.6 * math.exp(-0.3 * l)


def _params(*semantics):
    return pltpu.CompilerParams(dimension_semantics=semantics, vmem_limit_bytes=VMEM_LIMIT_BYTES)


def _rms_rows(x, gain):
    ms = jnp.mean(x * x, axis=-1, keepdims=True)
    return x * lax.rsqrt(ms + EPS) * gain


def _diff_lambda(lq1, lk1, lq2, lk2, lam_init):
    e1 = jnp.exp(jnp.sum(lq1[...] * lk1[...], axis=-1, keepdims=True))
    e2 = jnp.exp(jnp.sum(lq2[...] * lk2[...], axis=-1, keepdims=True))
    return e1 - e2 + lam_init


def _in_proj_kernel(x_ref, g_ref, w_ref, wg_ref, gm_ref, qg_ref, kg_ref,
                    q_ref, k_ref, v_ref, h_ref, xn_ref):
    j = pl.program_id(1)
    n_q = D_ATTN // IN_CHUNK

    @pl.when(j == 0)
    def _():
        xn_ref[...] = _rms_rows(x_ref[...], g_ref[...]).astype(BF16)

    xn = xn_ref[...]
    z = jnp.dot(xn, w_ref[...], preferred_element_type=F32)

    def head_norm(gain):
        ms = jnp.dot((z * z).astype(BF16), gm_ref[...], preferred_element_type=F32)
        return z * lax.rsqrt(ms + EPS) * gain

    @pl.when(j < n_q)
    def _():
        q_ref[...] = (head_norm(qg_ref[...]) * QK_SCALE).astype(BF16)

    @pl.when((j >= n_q) & (j < 2 * n_q))
    def _():
        k_ref[...] = head_norm(kg_ref[...])

    @pl.when((j >= 2 * n_q) & (j < 3 * n_q))
    def _():
        v_ref[...] = z

    @pl.when(j >= 3 * n_q)
    def _():
        gate = jnp.dot(xn, wg_ref[...], preferred_element_type=F32)
        h_ref[...] = z * jax.nn.sigmoid(gate)


def _in_proj(x, gain, w_bf16, gm, qg, kg, tm):
    m = x.shape[0]
    n_q = D_ATTN // IN_CHUNK
    n_steps = 4 * n_q
    gate0 = (3 * D_ATTN + D_CONV) // IN_CHUNK

    def region(r):
        return lambda i, j: (i, jnp.clip(j - r * n_q, 0, n_q - 1))

    return pl.pallas_call(
        _in_proj_kernel,
        grid=(m // tm, n_steps),
        in_specs=[
            pl.BlockSpec((tm, D_MODEL), lambda i, j: (i, 0)),
            pl.BlockSpec((1, D_MODEL), lambda i, j: (0, 0)),
            pl.BlockSpec((D_MODEL, IN_CHUNK), lambda i, j: (0, j)),
            pl.BlockSpec((D_MODEL, IN_CHUNK), lambda i, j: (0, gate0 + jnp.maximum(j - 3 * n_q, 0))),
            pl.BlockSpec((IN_CHUNK, IN_CHUNK), lambda i, j: (0, 0)),
            pl.BlockSpec((1, IN_CHUNK), lambda i, j: (0, 0)),
            pl.BlockSpec((1, IN_CHUNK), lambda i, j: (0, 0)),
        ],
        out_specs=[
            pl.BlockSpec((tm, IN_CHUNK), region(0)),
            pl.BlockSpec((tm, IN_CHUNK), region(1)),
            pl.BlockSpec((tm, IN_CHUNK), region(2)),
            pl.BlockSpec((tm, IN_CHUNK), region(3)),
        ],
        out_shape=[
            jax.ShapeDtypeStruct((m, D_ATTN), BF16),
            jax.ShapeDtypeStruct((m, D_ATTN), F32),
            jax.ShapeDtypeStruct((m, D_ATTN), F32),
            jax.ShapeDtypeStruct((m, D_CONV), F32),
        ],
        scratch_shapes=[pltpu.VMEM((tm, D_MODEL), BF16)],
        compiler_params=_params("parallel", "arbitrary"),
        name="in_proj",
    )(x, gain, w_bf16, w_bf16, gm, qg, kg)


def _sub_ln(o, gain, lam_init):
    ms = jnp.mean(o * o, axis=-1, keepdims=True)
    return o * lax.rsqrt(ms + EPS) * gain * (1.0 - lam_init)


def _scores_t(kc, qq):
    return lax.dot_general(kc, qq, (((1,), (1,)), ((), ())), preferred_element_type=F32)


def _softmax_t(s, m, l):
    m_new = jnp.maximum(m, jnp.max(s, axis=0, keepdims=True))
    alpha = jnp.exp(m - m_new)
    p = jnp.exp(s - m_new)
    return m_new, alpha, p, alpha * l + jnp.sum(p, axis=0, keepdims=True)


def _attn_prompt_kernel(lq1, lk1, lq2, lk2, sgt_ref, q_ref, k_ref, v_ref, o_ref,
                        kb_ref, vt_ref, *, seq, lam_init):
    lam = _diff_lambda(lq1, lk1, lq2, lk2, lam_init)
    ct = ATTN_TILE
    n_full, tail = divmod(seq, ct)
    n_chunks = kb_ref.shape[0] // ct
    pad = n_chunks * ct - seq
    kb_ref[pl.ds(0, seq), :] = k_ref[0].astype(BF16)
    kb_ref[pl.ds(seq, pad), :] = jnp.zeros((pad, PAIR_DIM), BF16)
    for c in range(n_chunks):
        if c < n_full:
            vc = v_ref[0, pl.ds(c * ct, ct), :]
        else:
            vc = jnp.concatenate([v_ref[0, pl.ds(c * ct, tail), :], jnp.zeros((ct - tail, PAIR_DIM), F32)], axis=0)
        vt_ref[c] = vc.T.astype(BF16)

    first = lax.broadcasted_iota(jnp.int32, (1, PAIR_DIM), 1) < HEAD_DIM
    for i in range(n_chunks):
        r0 = i * ct
        if i < n_full:
            t, valid = ct, ct
            qt = q_ref[0, pl.ds(r0, t), :]
        else:
            t, valid = ATTN_TAIL_TILE, tail
            qt = jnp.concatenate([q_ref[0, pl.ds(r0, tail), :], jnp.zeros((t - tail, PAIR_DIM), BF16)], axis=0)
        zero = jnp.zeros_like(qt)
        qq = jnp.concatenate([jnp.where(first, qt, zero), jnp.where(first, zero, qt)], axis=0)
        def scores(c, qq=qq):
            c0 = c * ct if isinstance(c, int) else pl.multiple_of(c * ct, ct)
            return _scores_t(kb_ref[pl.ds(c0, ct), :], qq)

        carry = (scores(0), jnp.full((1, 2 * t), NEG_INF, F32), jnp.zeros((1, 2 * t), F32),
                 jnp.zeros((PAIR_DIM, 2 * t), F32), jnp.zeros((ct, 2 * t), BF16))

        def body(c, carry, scores=scores):
            s, m, l, acc, p_prev = carry
            pv = jnp.dot(vt_ref[jnp.maximum(c - 1, 0)], p_prev, preferred_element_type=F32)
            s_next = scores(c + 1)
            m, alpha, p, l = _softmax_t(s, m, l)
            return s_next, m, l, alpha * (acc + pv), p.astype(BF16)

        if i > 0:
            carry = lax.fori_loop(0, i, body, carry)
        s, m, l, acc, p_prev = carry
        pv = jnp.dot(vt_ref[max(i - 1, 0)], p_prev, preferred_element_type=F32)
        kpos = lax.broadcasted_iota(jnp.int32, (ct, 2 * t), 0)
        qpos = lax.broadcasted_iota(jnp.int32, (ct, 2 * t), 1) % t
        _, alpha, p, l = _softmax_t(jnp.where(kpos <= qpos, s, NEG_INF), m, l)
        acc = alpha * (acc + pv) + jnp.dot(vt_ref[i], p.astype(BF16), preferred_element_type=F32)
        ot = acc[:, :t] / l[:, :t] - lam * (acc[:, t:] / l[:, t:])
        ms = jnp.mean(ot * ot, axis=0, keepdims=True)
        ot = ot * lax.rsqrt(ms + EPS) * sgt_ref[...] * (1.0 - lam_init)
        o_ref[0, pl.ds(r0, valid), :] = ot.T[:valid].astype(BF16)


def _attn_prompt(lams, sub_gain, q, k, v, batch, seq, lam_init):
    q3 = q.reshape(batch, seq, D_ATTN)
    k3 = k.reshape(batch, seq, D_ATTN)
    v3 = v.reshape(batch, seq, D_ATTN)
    n_chunks = pl.cdiv(seq, ATTN_TILE)
    assert seq % ATTN_TILE <= ATTN_TAIL_TILE
    vec = pl.BlockSpec((1, HEAD_DIM), lambda b, h: (0, 0))
    blk = pl.BlockSpec((1, seq, PAIR_DIM), lambda b, h: (b, 0, h))
    out = pl.pallas_call(
        functools.partial(_attn_prompt_kernel, seq=seq, lam_init=lam_init),
        grid=(batch, N_HEADS),
        in_specs=[vec, vec, vec, vec, pl.BlockSpec((PAIR_DIM, 1), lambda b, h: (0, 0)), blk, blk, blk],
        out_specs=blk,
        out_shape=jax.ShapeDtypeStruct((batch, seq, D_ATTN), BF16),
        scratch_shapes=[pltpu.VMEM((n_chunks * ATTN_TILE, PAIR_DIM), BF16),
                        pltpu.VMEM((n_chunks, PAIR_DIM, ATTN_TILE), BF16)],
        compiler_params=_params("parallel", "parallel"),
        name="attn_prompt",
    )(*lams, sub_gain.reshape(PAIR_DIM, 1), q3, k3, v3)
    return out.reshape(batch * seq, D_ATTN)


SAMPLE_ROWS = 8


def _attn_sample_kernel(pt_ref, lq1, lk1, lq2, lk2, sg_ref, qa_ref, qb_ref, kn_ref, vn_ref, *rest,
                        n_q, page, lam_init):
    del pt_ref
    npg = PAGES_PER_STEP
    k_refs, v_refs = rest[:npg], rest[npg:2 * npg]
    o_ref, m_ref, l_ref, acc_ref = rest[2 * npg:]
    s_idx = pl.program_id(1)
    rows = N_HEADS * SAMPLE_ROWS
    half = SAMPLE_ROWS // 2

    @pl.when(s_idx == 0)
    def _():
        m_ref[...] = jnp.full(m_ref.shape, NEG_INF, F32)
        l_ref[...] = jnp.zeros(l_ref.shape, F32)
        acc_ref[...] = jnp.zeros(acc_ref.shape, F32)

    qa = [qa_ref[0, h].astype(BF16) for h in range(N_HEADS)]
    qb = [qb_ref[0, h].astype(BF16) for h in range(N_HEADS)]

    def scores(k_head):
        blocks = []
        for h in range(N_HEADS):
            s1 = jnp.dot(qa[h], k_head(2 * h).astype(BF16), preferred_element_type=F32)
            s2 = jnp.dot(qb[h], k_head(2 * h + 1).astype(BF16), preferred_element_type=F32)
            blocks.append(s1 + s2)
        return jnp.concatenate(blocks, axis=0)

    def update(s, v_heads):
        m = m_ref[...]
        m_new = jnp.maximum(m, jnp.max(s, axis=-1, keepdims=True))
        alpha = jnp.exp(m - m_new)
        p = jnp.exp(s - m_new)
        l_ref[...] = alpha * l_ref[...] + jnp.sum(p, axis=-1, keepdims=True)
        pv = []
        for h in range(N_HEADS):
            ph = p[h * SAMPLE_ROWS:(h + 1) * SAMPLE_ROWS]
            acc = None
            for i, v_head in enumerate(v_heads):
                d = jnp.dot(ph[:, i * page:(i + 1) * page].astype(BF16), v_head(h).astype(BF16),
                            preferred_element_type=F32)
                acc = d if acc is None else acc + d
            pv.append(acc)
        acc_ref[...] = alpha * acc_ref[...] + jnp.concatenate(pv, axis=0)
        m_ref[...] = m_new

    def k_slice(ref, lead):
        return lambda h16: ref[(*lead, h16)]

    def v_slice(ref, lead):

        return lambda h: ref[(*lead, pl.ds(h, page, stride=N_HEADS), slice(None))]

    s = jnp.concatenate([scores(k_slice(kr, (0, 0))) for kr in k_refs], axis=1)
    update(s, [v_slice(vr, (0, 0)) for vr in v_refs])

    @pl.when(s_idx == pl.num_programs(1) - 1)
    def _():
        qi = lax.broadcasted_iota(jnp.int32, (rows, page), 0) % half
        kj = lax.broadcasted_iota(jnp.int32, (rows, page), 1)
        s_new = scores(k_slice(kn_ref, (0,)))
        update(jnp.where(kj <= qi, s_new, NEG_INF), [v_slice(vn_ref, (0,))])

        f = acc_ref[...] / l_ref[...]
        lam = _diff_lambda(lq1, lk1, lq2, lk2, lam_init)
        for h in range(N_HEADS):
            r0 = h * SAMPLE_ROWS
            o = f[r0:r0 + SAMPLE_ROWS] - lam * pltpu.roll(f[r0:r0 + SAMPLE_ROWS], half, 0)
            o_ref[0, h] = _sub_ln(o, sg_ref[...], lam_init)


def _attn_sample(page_table, lams, sub_gain, q, k_new, v_new, cache_k, cache_v, layer, lam_init):
    db, n_pages = page_table.shape
    n_q = q.shape[0] // db
    page = cache_k.shape[2]
    npg = PAGES_PER_STEP
    rows = N_HEADS * SAMPLE_ROWS
    half = SAMPLE_ROWS // 2
    assert n_q <= half

    q5 = q.astype(F32).reshape(db, n_q, N_HEADS, 2, HEAD_DIM).transpose(0, 2, 3, 1, 4)
    q5 = jnp.pad(q5, ((0, 0), (0, 0), (0, 0), (0, half - n_q), (0, 0)))
    zeros = jnp.zeros_like(q5[:, :, 0])
    qa = jnp.concatenate([q5[:, :, 0], zeros], axis=2)
    qb = jnp.concatenate([zeros, q5[:, :, 1]], axis=2)
    v_rows = page * N_HEADS
    kn = jnp.pad(k_new.reshape(db, n_q, 2 * N_HEADS, HEAD_DIM).transpose(0, 2, 3, 1),
                 ((0, 0), (0, 0), (0, 0), (0, page - n_q)))
    vn = jnp.pad(v_new.reshape(db, n_q * N_HEADS, PAIR_DIM), ((0, 0), (0, v_rows - n_q * N_HEADS), (0, 0)))
    ck = cache_k.transpose(0, 1, 3, 4, 2)
    cv = cache_v.reshape(cache_v.shape[0], cache_v.shape[1], v_rows, PAIR_DIM)

    vec = pl.BlockSpec((1, HEAD_DIM), lambda b, s, pt: (0, 0))
    qspec = pl.BlockSpec((1, N_HEADS, SAMPLE_ROWS, HEAD_DIM), lambda b, s, pt: (b, 0, 0, 0))
    k_page = (2 * N_HEADS, HEAD_DIM, page)

    def k_spec(p):
        return pl.BlockSpec((1, 1) + k_page, lambda b, s, pt: (layer, pt[b, s * npg + p], 0, 0, 0))

    def v_spec(p):
        return pl.BlockSpec((1, 1, v_rows, PAIR_DIM), lambda b, s, pt: (layer, pt[b, s * npg + p], 0, 0))

    grid_spec = pltpu.PrefetchScalarGridSpec(
        num_scalar_prefetch=1,
        grid=(db, n_pages // npg),
        in_specs=[vec, vec, vec, vec, pl.BlockSpec((1, PAIR_DIM), lambda b, s, pt: (0, 0)), qspec, qspec,
                  pl.BlockSpec((1,) + k_page, lambda b, s, pt: (b, 0, 0, 0)),
                  pl.BlockSpec((1, v_rows, PAIR_DIM), lambda b, s, pt: (b, 0, 0))]
                 + [k_spec(p) for p in range(npg)] + [v_spec(p) for p in range(npg)],
        out_specs=pl.BlockSpec((1, N_HEADS, SAMPLE_ROWS, PAIR_DIM), lambda b, s, pt: (b, 0, 0, 0)),
        scratch_shapes=[pltpu.VMEM((rows, 1), F32), pltpu.VMEM((rows, 1), F32),
                        pltpu.VMEM((rows, PAIR_DIM), F32)],
    )
    out = pl.pallas_call(
        functools.partial(_attn_sample_kernel, n_q=n_q, page=page, lam_init=lam_init),
        grid_spec=grid_spec,
        out_shape=jax.ShapeDtypeStruct((db, N_HEADS, SAMPLE_ROWS, PAIR_DIM), F32),
        compiler_params=_params("parallel", "arbitrary"),
        name="attn_sample",
    )(page_table, *lams, sub_gain, qa, qb, kn, vn, *([ck] * npg), *([cv] * npg))
    return out[:, :, :n_q].transpose(0, 2, 1, 3).reshape(db * n_q, D_ATTN).astype(BF16)


def _ln_silu(cf, g, b):
    mu = jnp.mean(cf, axis=-1, keepdims=True)
    d = cf - mu
    var = jnp.mean(d * d, axis=-1, keepdims=True)
    cn = d * lax.rsqrt(var + EPS) * g + b
    return cn * jax.nn.sigmoid(cn)


def _conv_prompt_kernel(h_ref, w_ref, b_ref, g_ref, bb_ref, o_ref, ext_ref, cv_ref, *, seq):
    lanes = 128
    ext_ref[pl.ds(0, CONV_HALO), :] = jnp.zeros((CONV_HALO, D_CONV), F32)
    ext_ref[pl.ds(CONV_HALO, seq), :] = h_ref[0]
    shift0 = CONV_HALO - (CONV_W - 1)
    win = CONV_ROWS + CONV_HALO

    def tile(it, carry):
        t0 = pl.multiple_of(it * CONV_ROWS, 8)
        for c in range(D_CONV // lanes):
            cols = pl.ds(c * lanes, lanes)
            w = ext_ref[pl.ds(t0, win), cols]
            acc = jnp.zeros((CONV_ROWS, lanes), F32)
            for r in range(8):
                n_a = (CONV_W - 1 + shift0 - r) // 8 + 1
                e = w[r:r + 8 * (n_a - 1) + CONV_ROWS]
                for a in range(n_a):
                    tap = 8 * a + r - shift0
                    if tap < 0:
                        continue
                    acc = acc + e[8 * a:8 * a + CONV_ROWS] * w_ref[pl.ds(tap, 1), cols]
            cv_ref[:, cols] = acc
        cf = cv_ref[...] + b_ref[...]
        o_ref[0, pl.ds(t0, CONV_ROWS), :] = _ln_silu(cf, g_ref[...], bb_ref[...]).astype(BF16)
        return carry

    lax.fori_loop(0, seq // CONV_ROWS, tile, 0)


def _conv_prompt(h, dw_w, dw_b, ln_g, ln_b, batch, seq):
    h3 = h.reshape(batch, seq, D_CONV)
    row = pl.BlockSpec((1, D_CONV), lambda b: (0, 0))
    out = pl.pallas_call(
        functools.partial(_conv_prompt_kernel, seq=seq),
        grid=(batch,),
        in_specs=[pl.BlockSpec((1, seq, D_CONV), lambda b: (b, 0, 0)),
                  pl.BlockSpec((CONV_W, D_CONV), lambda b: (0, 0)), row, row, row],
        out_specs=pl.BlockSpec((1, seq, D_CONV), lambda b: (b, 0, 0)),
        out_shape=jax.ShapeDtypeStruct((batch, seq, D_CONV), BF16),
        scratch_shapes=[pltpu.VMEM((CONV_HALO + seq, D_CONV), F32), pltpu.VMEM((CONV_ROWS, D_CONV), F32)],
        compiler_params=_params("parallel"),
        name="conv_prompt",
    )(h3, dw_w, dw_b, ln_g, ln_b)
    return out.reshape(batch * seq, D_CONV)


def _conv_sample_kernel(ext_ref, w_ref, b_ref, g_ref, bb_ref, o_ref, *, n_q):
    w = w_ref[...]
    for t in range(n_q):
        cf = jnp.sum(ext_ref[0, pl.ds(t, CONV_W), :] * w, axis=0, keepdims=True) + b_ref[...]
        o_ref[0, pl.ds(t, 1), :] = _ln_silu(cf, g_ref[...], bb_ref[...])


def _conv_sample(ext, dw_w, dw_b, ln_g, ln_b, n_q):
    db, rows, _ = ext.shape
    row = pl.BlockSpec((1, D_CONV), lambda b: (0, 0))
    out = pl.pallas_call(
        functools.partial(_conv_sample_kernel, n_q=n_q),
        grid=(db,),
        in_specs=[pl.BlockSpec((1, rows, D_CONV), lambda b: (b, 0, 0)),
                  pl.BlockSpec((CONV_W, D_CONV), lambda b: (0, 0)), row, row, row],
        out_specs=pl.BlockSpec((1, n_q, D_CONV), lambda b: (b, 0, 0)),
        out_shape=jax.ShapeDtypeStruct((db, n_q, D_CONV), F32),
        compiler_params=_params("parallel"),
        name="conv_sample",
    )(ext, dw_w, dw_b, ln_g, ln_b)
    return out.reshape(db * n_q, D_CONV).astype(BF16)


def _out_proj_kernel(x_ref, oa_ref, c_ref, wa_ref, wc_ref, o_ref):
    o_ref[...] = (x_ref[...]
                  + jnp.dot(oa_ref[...], wa_ref[...], preferred_element_type=F32)
                  + jnp.dot(c_ref[...], wc_ref[...], preferred_element_type=F32))


def _out_proj(x, oa, c, w_bf16, tm):
    m = x.shape[0]
    tn = D_MODEL // 2
    return pl.pallas_call(
        _out_proj_kernel,
        grid=(m // tm, D_MODEL // tn),
        in_specs=[pl.BlockSpec((tm, tn), lambda i, j: (i, j)),
                  pl.BlockSpec((tm, D_ATTN), lambda i, j: (i, 0)),
                  pl.BlockSpec((tm, D_CONV), lambda i, j: (i, 0)),
                  pl.BlockSpec((D_ATTN, tn), lambda i, j: (0, j)),
                  pl.BlockSpec((D_CONV, tn), lambda i, j: (D_ATTN // D_CONV, j))],
        out_specs=pl.BlockSpec((tm, tn), lambda i, j: (i, j)),
        out_shape=jax.ShapeDtypeStruct((m, D_MODEL), F32),
        compiler_params=_params("parallel", "arbitrary"),
        name="out_proj",
    )(x, oa, c, w_bf16, w_bf16)


def _ffn_kernel(x_ref, g_ref, wu_ref, wd_ref, o_ref, xn_ref):
    @pl.when(pl.program_id(1) == 0)
    def _():
        x = x_ref[...]
        xn_ref[...] = _rms_rows(x, g_ref[...]).astype(BF16)
        o_ref[...] = x

    h = jnp.dot(xn_ref[...], wu_ref[...], preferred_element_type=F32)
    a = jnp.square(jnp.maximum(h, 0.0)).astype(BF16)
    o_ref[...] += jnp.dot(a, wd_ref[...], preferred_element_type=F32)


def _ffn(x, gain, wu_bf16, wd_bf16, tm, tf):
    m = x.shape[0]
    return pl.pallas_call(
        _ffn_kernel,
        grid=(m // tm, D_FF // tf),
        in_specs=[pl.BlockSpec((tm, D_MODEL), lambda i, j: (i, 0)),
                  pl.BlockSpec((1, D_MODEL), lambda i, j: (0, 0)),
                  pl.BlockSpec((D_MODEL, tf), lambda i, j: (0, j)),
                  pl.BlockSpec((tf, D_MODEL), lambda i, j: (j, 0))],
        out_specs=pl.BlockSpec((tm, D_MODEL), lambda i, j: (i, 0)),
        out_shape=jax.ShapeDtypeStruct((m, D_MODEL), F32),
        scratch_shapes=[pltpu.VMEM((tm, D_MODEL), BF16)],
        compiler_params=_params("parallel", "arbitrary"),
        name="ffn",
    )(x, gain, wu_bf16, wd_bf16)


def _row_tile(m):
    for tm in (1032, 1024, 688, 512, 344, 256, 128, 64, 32, 16, 8):
        if m % tm == 0:
            return tm
    raise ValueError(f"unsupported row count {m}")


def kernel(x_prompt, x_sample, cache_k, cache_v, state_conv, page_table, meta_tokens, attn_norm, w_in,
           q_norm, k_norm, lambda_q1, lambda_k1, lambda_q2, lambda_k2, attn_subln, dw_w, dw_b,
           conv_ln_g, conv_ln_b, w_out, mlp_norm, w_up, w_down):
    depth = w_in.shape[0]
    b, seq_real, _ = x_prompt.shape
    seq = N_META + seq_real
    db, ds, _ = x_sample.shape

    w_in_b, w_out_b = w_in.astype(BF16), w_out.astype(BF16)
    w_up_b, w_down_b = w_up.astype(BF16), w_down.astype(BF16)
    heads_per_chunk = IN_CHUNK // HEAD_DIM
    blk = jnp.arange(IN_CHUNK) // HEAD_DIM
    gm = jnp.where(blk[:, None] == blk[None, :], 1.0 / HEAD_DIM, 0.0).astype(BF16)

    meta = jnp.broadcast_to(meta_tokens.astype(x_prompt.dtype)[None], (b, N_META, D_MODEL))
    xp = jnp.concatenate([meta, x_prompt], axis=1).reshape(b * seq, D_MODEL)
    xs = x_sample.reshape(db * ds, D_MODEL)
    tm_p, tm_s = _row_tile(b * seq), _row_tile(db * ds)

    kp, vp, cp, ks_, vs_, cs_ = [], [], [], [], [], []
    for l in range(depth):
        lam0 = lambda_init(l)
        lams = [a[l][None, :] for a in (lambda_q1, lambda_k1, lambda_q2, lambda_k2)]
        sub_gain = attn_subln[l][None, :]
        an, mn = attn_norm[l][None, :], mlp_norm[l][None, :]
        qg = jnp.tile(q_norm[l], heads_per_chunk)[None, :]
        kg = jnp.tile(k_norm[l], heads_per_chunk)[None, :]
        cw, cb = dw_w[l], dw_b[l][None, :]
        lg, lb = conv_ln_g[l][None, :], conv_ln_b[l][None, :]

        q, k, v, h = _in_proj(xp, an, w_in_b[l], gm, qg, kg, tm_p)
        oa = _attn_prompt(lams, sub_gain, q, k, v, b, seq, lam0)
        c = _conv_prompt(h, cw, cb, lg, lb, b, seq)
        x1 = _out_proj(xp, oa, c, w_out_b[l], tm_p)
        xp = _ffn(x1, mn, w_up_b[l], w_down_b[l], tm_p, 512)
        kp.append(k.reshape(b, seq, 2 * N_HEADS, HEAD_DIM))
        vp.append(v.reshape(b, seq, N_HEADS, PAIR_DIM))
        cp.append(h.reshape(b, seq, D_CONV)[:, seq - (CONV_W - 1):])

        q, k, v, h = _in_proj(xs, an, w_in_b[l], gm, qg, kg, tm_s)
        oa = _attn_sample(page_table, lams, sub_gain, q, k, v, cache_k, cache_v, l, lam0)
        ext = jnp.concatenate([state_conv[l], h.reshape(db, ds, D_CONV)], axis=1)
        c = _conv_sample(ext, cw, cb, lg, lb, ds)
        x1 = _out_proj(xs, oa, c, w_out_b[l], tm_s)
        xs = _ffn(x1, mn, w_up_b[l], w_down_b[l], tm_s, 512)
        ks_.append(k.reshape(db, ds, 2 * N_HEADS, HEAD_DIM))
        vs_.append(v.reshape(db, ds, N_HEADS, PAIR_DIM))
        cs_.append(ext[:, ds:])

    y_prompt = xp.reshape(b, seq, D_MODEL)[:, N_META:]
    y_sample = xs.reshape(db, ds, D_MODEL)
    return (y_prompt, y_sample, jnp.stack(kp), jnp.stack(vp), jnp.stack(cp),
            jnp.stack(ks_), jnp.stack(vs_), jnp.stack(cs_))
```

```python
import functools
import math

import jax
import jax.numpy as jnp
from jax import lax
from jax.experimental import pallas as pl
from jax.experimental.pallas import tpu as pltpu

F32 = jnp.float32
BF16 = jnp.bfloat16

D_MODEL = 2048
N_META = 16
D_ATTN = D_MODEL // 2
N_HEADS = 8
HEAD_DIM = D_ATTN // (2 * N_HEADS)
PAIR_DIM = 2 * HEAD_DIM
D_CONV = D_MODEL - D_ATTN
CONV_W = 31
D_FF = 4 * D_MODEL
D_IN = 3 * D_ATTN + 2 * D_CONV
EPS = 1e-6
NEG_INF = -1e30
QK_SCALE = HEAD_DIM ** -0.5

VMEM_LIMIT_BYTES = 56 * 1024 * 1024

IN_CHUNK = 512
ATTN_TILE = 256
ATTN_CHUNK = 256
ATTN_TAIL_TILE = 128
CONV_ROWS = 48
CONV_HALO = 32
PAGES_PER_STEP = 8


def lambda_init(l):
    return 0.8 - 0.6 * math.exp(-0.3 * l)


def _params(*semantics):
    return pltpu.CompilerParams(dimension_semantics=semantics, vmem_limit_bytes=VMEM_LIMIT_BYTES)


def _rms_rows(x, gain):
    ms = jnp.mean(x * x, axis=-1, keepdims=True)
    return x * lax.rsqrt(ms + EPS) * gain


def _diff_lambda(lq1, lk1, lq2, lk2, lam_init):
    e1 = jnp.exp(jnp.sum(lq1[...] * lk1[...], axis=-1, keepdims=True))
    e2 = jnp.exp(jnp.sum(lq2[...] * lk2[...], axis=-1, keepdims=True))
    return e1 - e2 + lam_init


def _in_proj_kernel(x_ref, g_ref, w_ref, wg_ref, gm_ref, qg_ref, kg_ref,
                    q_ref, k_ref, v_ref, h_ref, xn_ref):
    j = pl.program_id(1)
    n_q = D_ATTN // IN_CHUNK

    @pl.when(j == 0)
    def _():
        xn_ref[...] = _rms_rows(x_ref[...], g_ref[...]).astype(BF16)

    xn = xn_ref[...]
    z = jnp.dot(xn, w_ref[...], preferred_element_type=F32)

    def head_norm(gain):
        ms = jnp.dot((z * z).astype(BF16), gm_ref[...], preferred_element_type=F32)
        return z * lax.rsqrt(ms + EPS) * gain

    @pl.when(j < n_q)
    def _():
        q_ref[...] = (head_norm(qg_ref[...]) * QK_SCALE).astype(BF16)

    @pl.when((j >= n_q) & (j < 2 * n_q))
    def _():
        k_ref[...] = head_norm(kg_ref[...])

    @pl.when((j >= 2 * n_q) & (j < 3 * n_q))
    def _():
        v_ref[...] = z

    @pl.when(j >= 3 * n_q)
    def _():
        gate = jnp.dot(xn, wg_ref[...], preferred_element_type=F32)
        h_ref[...] = z * jax.nn.sigmoid(gate)


def _in_proj(x, gain, w_bf16, layer, gm, qg, kg, tm):
    m = x.shape[0]
    n_q = D_ATTN // IN_CHUNK
    n_steps = 4 * n_q
    gate0 = (3 * D_ATTN + D_CONV) // IN_CHUNK

    def region(r):
        return lambda i, j: (i, jnp.clip(j - r * n_q, 0, n_q - 1))

    return pl.pallas_call(
        _in_proj_kernel,
        grid=(m // tm, n_steps),
        in_specs=[
            pl.BlockSpec((tm, D_MODEL), lambda i, j: (i, 0)),
            pl.BlockSpec((1, D_MODEL), lambda i, j: (0, 0)),
            pl.BlockSpec((None, D_MODEL, IN_CHUNK), lambda i, j: (layer, 0, j)),
            pl.BlockSpec((None, D_MODEL, IN_CHUNK), lambda i, j: (layer, 0, gate0 + jnp.maximum(j - 3 * n_q, 0))),
            pl.BlockSpec((IN_CHUNK, IN_CHUNK), lambda i, j: (0, 0)),
            pl.BlockSpec((1, IN_CHUNK), lambda i, j: (0, 0)),
            pl.BlockSpec((1, IN_CHUNK), lambda i, j: (0, 0)),
        ],
        out_specs=[
            pl.BlockSpec((tm, IN_CHUNK), region(0)),
            pl.BlockSpec((tm, IN_CHUNK), region(1)),
            pl.BlockSpec((tm, IN_CHUNK), region(2)),
            pl.BlockSpec((tm, IN_CHUNK), region(3)),
        ],
        out_shape=[
            jax.ShapeDtypeStruct((m, D_ATTN), BF16),
            jax.ShapeDtypeStruct((m, D_ATTN), F32),
            jax.ShapeDtypeStruct((m, D_ATTN), F32),
            jax.ShapeDtypeStruct((m, D_CONV), F32),
        ],
        scratch_shapes=[pltpu.VMEM((tm, D_MODEL), BF16)],
        compiler_params=_params("parallel", "arbitrary"),
        name="in_proj",
    )(x, gain, w_bf16, w_bf16, gm, qg, kg)


def _sub_ln(o, gain, lam_init):
    ms = jnp.mean(o * o, axis=-1, keepdims=True)
    return o * lax.rsqrt(ms + EPS) * gain * (1.0 - lam_init)


def _scores_t(kc, qq):
    return lax.dot_general(kc, qq, (((1,), (1,)), ((), ())), preferred_element_type=F32)


def _softmax_t(s, m, l):
    m_new = jnp.maximum(m, jnp.max(s, axis=0, keepdims=True))
    alpha = jnp.exp(m - m_new)
    p = jnp.exp(s - m_new)
    return m_new, alpha, p, alpha * l + jnp.sum(p, axis=0, keepdims=True)


def _attn_prompt_kernel(lq1, lk1, lq2, lk2, sgt_ref, q_ref, k_ref, v_ref, o_ref,
                        kb_ref, vt_ref, *, seq, lam_init):
    lam = _diff_lambda(lq1, lk1, lq2, lk2, lam_init)
    ct, tq = ATTN_CHUNK, ATTN_TILE
    n_chunks = kb_ref.shape[0] // ct
    pad = n_chunks * ct - seq
    kb_ref[pl.ds(0, seq), :] = k_ref[0].astype(BF16)
    kb_ref[pl.ds(seq, pad), :] = jnp.zeros((pad, PAIR_DIM), BF16)
    for c in range(n_chunks):
        rows = min(ct, seq - c * ct)
        vc = v_ref[0, pl.ds(c * ct, rows), :]
        if rows < ct:
            vc = jnp.concatenate([vc, jnp.zeros((ct - rows, PAIR_DIM), F32)], axis=0)
        vt_ref[c] = vc.T.astype(BF16)

    first = lax.broadcasted_iota(jnp.int32, (1, PAIR_DIM), 1) < HEAD_DIM
    n_full, tail = divmod(seq, tq)
    for i in range(pl.cdiv(seq, tq)):
        r0 = i * tq
        n_before = r0 // ct
        if i < n_full:
            t, valid = tq, tq
            qt = q_ref[0, pl.ds(r0, t), :]
        else:
            t, valid = ATTN_TAIL_TILE, tail
            qt = jnp.concatenate([q_ref[0, pl.ds(r0, tail), :], jnp.zeros((t - tail, PAIR_DIM), BF16)], axis=0)
        zero = jnp.zeros_like(qt)
        qq = jnp.concatenate([jnp.where(first, qt, zero), jnp.where(first, zero, qt)], axis=0)
        def scores(c, qq=qq):
            c0 = c * ct if isinstance(c, int) else pl.multiple_of(c * ct, ct)
            return _scores_t(kb_ref[pl.ds(c0, ct), :], qq)

        carry = (scores(0), jnp.full((1, 2 * t), NEG_INF, F32), jnp.zeros((1, 2 * t), F32),
                 jnp.zeros((PAIR_DIM, 2 * t), F32), jnp.zeros((ct, 2 * t), BF16))

        def body(c, carry, scores=scores):
            s, m, l, acc, p_prev = carry
            pv = jnp.dot(vt_ref[jnp.maximum(c - 1, 0)], p_prev, preferred_element_type=F32)
            s_next = scores(c + 1)
            m, alpha, p, l = _softmax_t(s, m, l)
            return s_next, m, l, alpha * (acc + pv), p.astype(BF16)

        if n_before > 0:
            carry = lax.fori_loop(0, n_before, body, carry)
        s, m, l, acc, p_prev = carry
        pv = jnp.dot(vt_ref[max(n_before - 1, 0)], p_prev, preferred_element_type=F32)
        kpos = n_before * ct + lax.broadcasted_iota(jnp.int32, (ct, 2 * t), 0)
        qpos = r0 + lax.broadcasted_iota(jnp.int32, (ct, 2 * t), 1) % t
        _, alpha, p, l = _softmax_t(jnp.where(kpos <= qpos, s, NEG_INF), m, l)
        acc = alpha * (acc + pv) + jnp.dot(vt_ref[n_before], p.astype(BF16), preferred_element_type=F32)
        ot = acc[:, :t] / l[:, :t] - lam * (acc[:, t:] / l[:, t:])
        ms = jnp.mean(ot * ot, axis=0, keepdims=True)
        ot = ot * lax.rsqrt(ms + EPS) * sgt_ref[...] * (1.0 - lam_init)
        o_ref[0, pl.ds(r0, valid), :] = ot.T[:valid].astype(BF16)


def _attn_prompt(lams, sub_gain, q, k, v, batch, seq, lam_init):
    q3 = q.reshape(batch, seq, D_ATTN)
    k3 = k.reshape(batch, seq, D_ATTN)
    v3 = v.reshape(batch, seq, D_ATTN)
    n_chunks = pl.cdiv(seq, ATTN_CHUNK)
    assert seq % ATTN_TILE <= ATTN_TAIL_TILE and ATTN_CHUNK % ATTN_TILE == 0
    vec = pl.BlockSpec((1, HEAD_DIM), lambda b, h: (0, 0))
    blk = pl.BlockSpec((1, seq, PAIR_DIM), lambda b, h: (b, 0, h))
    out = pl.pallas_call(
        functools.partial(_attn_prompt_kernel, seq=seq, lam_init=lam_init),
        grid=(batch, N_HEADS),
        in_specs=[vec, vec, vec, vec, pl.BlockSpec((PAIR_DIM, 1), lambda b, h: (0, 0)), blk, blk, blk],
        out_specs=blk,
        out_shape=jax.ShapeDtypeStruct((batch, seq, D_ATTN), BF16),
        scratch_shapes=[pltpu.VMEM((n_chunks * ATTN_CHUNK, PAIR_DIM), BF16),
                        pltpu.VMEM((n_chunks, PAIR_DIM, ATTN_CHUNK), BF16)],
        compiler_params=_params("parallel", "parallel"),
        name="attn_prompt",
    )(*lams, sub_gain.reshape(PAIR_DIM, 1), q3, k3, v3)
    return out.reshape(batch * seq, D_ATTN)


SAMPLE_ROWS = 8


def _attn_sample_kernel(pt_ref, lq1, lk1, lq2, lk2, sg_ref, qa_ref, qb_ref, kn_ref, vn_ref, *rest,
                        n_q, page, lam_init):
    del pt_ref
    npg = PAGES_PER_STEP
    k_refs, v_refs = rest[:npg], rest[npg:2 * npg]
    o_ref, m_ref, l_ref, acc_ref = rest[2 * npg:]
    s_idx = pl.program_id(1)
    rows = N_HEADS * SAMPLE_ROWS
    half = SAMPLE_ROWS // 2

    @pl.when(s_idx == 0)
    def _():
        m_ref[...] = jnp.full(m_ref.shape, NEG_INF, F32)
        l_ref[...] = jnp.zeros(l_ref.shape, F32)
        acc_ref[...] = jnp.zeros(acc_ref.shape, F32)

    qa = [qa_ref[0, h].astype(BF16) for h in range(N_HEADS)]
    qb = [qb_ref[0, h].astype(BF16) for h in range(N_HEADS)]

    def scores(k_head):
        blocks = []
        for h in range(N_HEADS):
            s1 = jnp.dot(qa[h], k_head(2 * h).astype(BF16), preferred_element_type=F32)
            s2 = jnp.dot(qb[h], k_head(2 * h + 1).astype(BF16), preferred_element_type=F32)
            blocks.append(s1 + s2)
        return jnp.concatenate(blocks, axis=0)

    def update(s, v_heads):
        m = m_ref[...]
        m_new = jnp.maximum(m, jnp.max(s, axis=-1, keepdims=True))
        alpha = jnp.exp(m - m_new)
        p = jnp.exp(s - m_new)
        l_ref[...] = alpha * l_ref[...] + jnp.sum(p, axis=-1, keepdims=True)
        pv = []
        for h in range(N_HEADS):
            ph = p[h * SAMPLE_ROWS:(h + 1) * SAMPLE_ROWS]
            acc = None
            for i, v_head in enumerate(v_heads):
                d = jnp.dot(ph[:, i * page:(i + 1) * page].astype(BF16), v_head(h).astype(BF16),
                            preferred_element_type=F32)
                acc = d if acc is None else acc + d
            pv.append(acc)
        acc_ref[...] = alpha * acc_ref[...] + jnp.concatenate(pv, axis=0)
        m_ref[...] = m_new

    def k_slice(ref, lead):
        return lambda h16: ref[(*lead, h16)]

    def v_slice(ref, lead):

        return lambda h: ref[(*lead, pl.ds(h, page, stride=N_HEADS), slice(None))]

    s = jnp.concatenate([scores(k_slice(kr, (0, 0))) for kr in k_refs], axis=1)
    update(s, [v_slice(vr, (0, 0)) for vr in v_refs])

    @pl.when(s_idx == pl.num_programs(1) - 1)
    def _():
        qi = lax.broadcasted_iota(jnp.int32, (rows, page), 0) % half
        kj = lax.broadcasted_iota(jnp.int32, (rows, page), 1)
        s_new = scores(k_slice(kn_ref, (0,)))
        update(jnp.where(kj <= qi, s_new, NEG_INF), [v_slice(vn_ref, (0,))])

        f = acc_ref[...] / l_ref[...]
        lam = _diff_lambda(lq1, lk1, lq2, lk2, lam_init)
        for h in range(N_HEADS):
            r0 = h * SAMPLE_ROWS
            o = f[r0:r0 + SAMPLE_ROWS] - lam * pltpu.roll(f[r0:r0 + SAMPLE_ROWS], half, 0)
            o_ref[0, h] = _sub_ln(o, sg_ref[...], lam_init)


def _attn_sample(page_table, lams, sub_gain, q, k_new, v_new, cache_k, cache_v, layer, lam_init):
    db, n_pages = page_table.shape
    n_q = q.shape[0] // db
    page = cache_k.shape[2]
    npg = PAGES_PER_STEP
    rows = N_HEADS * SAMPLE_ROWS
    half = SAMPLE_ROWS // 2
    assert n_q <= half

    q5 = q.astype(F32).reshape(db, n_q, N_HEADS, 2, HEAD_DIM).transpose(0, 2, 3, 1, 4)
    q5 = jnp.pad(q5, ((0, 0), (0, 0), (0, 0), (0, half - n_q), (0, 0)))
    zeros = jnp.zeros_like(q5[:, :, 0])
    qa = jnp.concatenate([q5[:, :, 0], zeros], axis=2)
    qb = jnp.concatenate([zeros, q5[:, :, 1]], axis=2)
    v_rows = page * N_HEADS
    kn = jnp.pad(k_new.reshape(db, n_q, 2 * N_HEADS, HEAD_DIM).transpose(0, 2, 3, 1),
                 ((0, 0), (0, 0), (0, 0), (0, page - n_q)))
    vn = jnp.pad(v_new.reshape(db, n_q * N_HEADS, PAIR_DIM), ((0, 0), (0, v_rows - n_q * N_HEADS), (0, 0)))
    ck = cache_k.transpose(0, 1, 3, 4, 2)
    cv = cache_v.reshape(cache_v.shape[0], cache_v.shape[1], v_rows, PAIR_DIM)

    vec = pl.BlockSpec((1, HEAD_DIM), lambda b, s, pt: (0, 0))
    qspec = pl.BlockSpec((1, N_HEADS, SAMPLE_ROWS, HEAD_DIM), lambda b, s, pt: (b, 0, 0, 0))
    k_page = (2 * N_HEADS, HEAD_DIM, page)

    def k_spec(p):
        return pl.BlockSpec((1, 1) + k_page, lambda b, s, pt: (layer, pt[b, s * npg + p], 0, 0, 0))

    def v_spec(p):
        return pl.BlockSpec((1, 1, v_rows, PAIR_DIM), lambda b, s, pt: (layer, pt[b, s * npg + p], 0, 0))

    grid_spec = pltpu.PrefetchScalarGridSpec(
        num_scalar_prefetch=1,
        grid=(db, n_pages // npg),
        in_specs=[vec, vec, vec, vec, pl.BlockSpec((1, PAIR_DIM), lambda b, s, pt: (0, 0)), qspec, qspec,
                  pl.BlockSpec((1,) + k_page, lambda b, s, pt: (b, 0, 0, 0)),
                  pl.BlockSpec((1, v_rows, PAIR_DIM), lambda b, s, pt: (b, 0, 0))]
                 + [k_spec(p) for p in range(npg)] + [v_spec(p) for p in range(npg)],
        out_specs=pl.BlockSpec((1, N_HEADS, SAMPLE_ROWS, PAIR_DIM), lambda b, s, pt: (b, 0, 0, 0)),
        scratch_shapes=[pltpu.VMEM((rows, 1), F32), pltpu.VMEM((rows, 1), F32),
                        pltpu.VMEM((rows, PAIR_DIM), F32)],
    )
    out = pl.pallas_call(
        functools.partial(_attn_sample_kernel, n_q=n_q, page=page, lam_init=lam_init),
        grid_spec=grid_spec,
        out_shape=jax.ShapeDtypeStruct((db, N_HEADS, SAMPLE_ROWS, PAIR_DIM), F32),
        compiler_params=_params("parallel", "arbitrary"),
        name="attn_sample",
    )(page_table, *lams, sub_gain, qa, qb, kn, vn, *([ck] * npg), *([cv] * npg))
    return out[:, :, :n_q].transpose(0, 2, 1, 3).reshape(db * n_q, D_ATTN).astype(BF16)


def _ln_silu(cf, g, b):
    mu = jnp.mean(cf, axis=-1, keepdims=True)
    d = cf - mu
    var = jnp.mean(d * d, axis=-1, keepdims=True)
    cn = d * lax.rsqrt(var + EPS) * g + b
    return cn * jax.nn.sigmoid(cn)


def _conv_prompt_kernel(h_ref, w_ref, b_ref, g_ref, bb_ref, o_ref, ext_ref, sh_ref, *, seq):
    lanes = 128
    n_sh = sh_ref.shape[1]
    ext_ref[pl.ds(0, CONV_HALO), :] = jnp.zeros((CONV_HALO, D_CONV), F32)
    ext_ref[pl.ds(CONV_HALO, seq), :] = h_ref[0]
    ext_ref[pl.ds(CONV_HALO + seq, 8), :] = jnp.zeros((8, D_CONV), F32)
    shift0 = CONV_HALO - (CONV_W - 1)
    n_tiles = seq // CONV_ROWS

    for c in range(D_CONV // lanes):
        cols = pl.ds(c * lanes, lanes)
        for r in range(8):
            sh_ref[r] = ext_ref[pl.ds(r, n_sh), cols]

        def tile(it, carry, cols=cols):
            t0 = pl.multiple_of(it * CONV_ROWS, 8)
            acc = jnp.zeros((CONV_ROWS, lanes), F32)
            for tap in range(CONV_W):
                a, r = divmod(tap + shift0, 8)
                acc = acc + sh_ref[r, pl.ds(t0 + 8 * a, CONV_ROWS), :] * w_ref[pl.ds(tap, 1), cols]
            ext_ref[pl.ds(CONV_HALO + t0, CONV_ROWS), cols] = acc
            return carry

        lax.fori_loop(0, n_tiles, tile, 0, unroll=2)

    def ln_tile(it, carry):
        t0 = pl.multiple_of(it * CONV_ROWS, 8)
        cf = ext_ref[pl.ds(CONV_HALO + t0, CONV_ROWS), :] + b_ref[...]
        o_ref[0, pl.ds(t0, CONV_ROWS), :] = _ln_silu(cf, g_ref[...], bb_ref[...]).astype(BF16)
        return carry

    lax.fori_loop(0, n_tiles, ln_tile, 0, unroll=4)


def _conv_prompt(h, dw_w, dw_b, ln_g, ln_b, batch, seq):
    h3 = h.reshape(batch, seq, D_CONV)
    row = pl.BlockSpec((1, D_CONV), lambda b: (0, 0))
    out = pl.pallas_call(
        functools.partial(_conv_prompt_kernel, seq=seq),
        grid=(batch,),
        in_specs=[pl.BlockSpec((1, seq, D_CONV), lambda b: (b, 0, 0)),
                  pl.BlockSpec((CONV_W, D_CONV), lambda b: (0, 0)), row, row, row],
        out_specs=pl.BlockSpec((1, seq, D_CONV), lambda b: (b, 0, 0)),
        out_shape=jax.ShapeDtypeStruct((batch, seq, D_CONV), BF16),
        scratch_shapes=[pltpu.VMEM((CONV_HALO + seq + 8, D_CONV), F32),
                        pltpu.VMEM((8, CONV_HALO + seq, 128), F32)],
        compiler_params=_params("parallel"),
        name="conv_prompt",
    )(h3, dw_w, dw_b, ln_g, ln_b)
    return out.reshape(batch * seq, D_CONV)


def _conv_sample_kernel(ext_ref, w_ref, b_ref, g_ref, bb_ref, o_ref, *, n_q):
    w = w_ref[...]
    for t in range(n_q):
        cf = jnp.sum(ext_ref[0, pl.ds(t, CONV_W), :] * w, axis=0, keepdims=True) + b_ref[...]
        o_ref[0, pl.ds(t, 1), :] = _ln_silu(cf, g_ref[...], bb_ref[...])


def _conv_sample(ext, dw_w, dw_b, ln_g, ln_b, n_q):
    db, rows, _ = ext.shape
    row = pl.BlockSpec((1, D_CONV), lambda b: (0, 0))
    out = pl.pallas_call(
        functools.partial(_conv_sample_kernel, n_q=n_q),
        grid=(db,),
        in_specs=[pl.BlockSpec((1, rows, D_CONV), lambda b: (b, 0, 0)),
                  pl.BlockSpec((CONV_W, D_CONV), lambda b: (0, 0)), row, row, row],
        out_specs=pl.BlockSpec((1, n_q, D_CONV), lambda b: (b, 0, 0)),
        out_shape=jax.ShapeDtypeStruct((db, n_q, D_CONV), F32),
        compiler_params=_params("parallel"),
        name="conv_sample",
    )(ext, dw_w, dw_b, ln_g, ln_b)
    return out.reshape(db * n_q, D_CONV).astype(BF16)


def _out_proj_kernel(x_ref, oa_ref, c_ref, wa_ref, wc_ref, o_ref):
    o_ref[...] = (x_ref[...]
                  + jnp.dot(oa_ref[...], wa_ref[...], preferred_element_type=F32)
                  + jnp.dot(c_ref[...], wc_ref[...], preferred_element_type=F32))


def _out_proj(x, oa, c, w_bf16, layer, tm):
    m = x.shape[0]
    tn = D_MODEL // 2
    return pl.pallas_call(
        _out_proj_kernel,
        grid=(m // tm, D_MODEL // tn),
        in_specs=[pl.BlockSpec((tm, tn), lambda i, j: (i, j)),
                  pl.BlockSpec((tm, D_ATTN), lambda i, j: (i, 0)),
                  pl.BlockSpec((tm, D_CONV), lambda i, j: (i, 0)),
                  pl.BlockSpec((None, D_ATTN, tn), lambda i, j: (layer, 0, j)),
                  pl.BlockSpec((None, D_CONV, tn), lambda i, j: (layer, D_ATTN // D_CONV, j))],
        out_specs=pl.BlockSpec((tm, tn), lambda i, j: (i, j)),
        out_shape=jax.ShapeDtypeStruct((m, D_MODEL), F32),
        compiler_params=_params("parallel", "arbitrary"),
        name="out_proj",
    )(x, oa, c, w_bf16, w_bf16)


def _ffn_kernel(x_ref, g_ref, wu_ref, wd_ref, o_ref, xn_ref):
    @pl.when(pl.program_id(1) == 0)
    def _():
        x = x_ref[...]
        xn_ref[...] = _rms_rows(x, g_ref[...]).astype(BF16)
        o_ref[...] = x

    h = jnp.dot(xn_ref[...], wu_ref[...], preferred_element_type=F32)
    a = jnp.square(jnp.maximum(h, 0.0)).astype(BF16)
    o_ref[...] += jnp.dot(a, wd_ref[...], preferred_element_type=F32)


def _ffn(x, gain, wu_bf16, wd_bf16, layer, tm, tf):
    m = x.shape[0]
    return pl.pallas_call(
        _ffn_kernel,
        grid=(m // tm, D_FF // tf),
        in_specs=[pl.BlockSpec((tm, D_MODEL), lambda i, j: (i, 0)),
                  pl.BlockSpec((1, D_MODEL), lambda i, j: (0, 0)),
                  pl.BlockSpec((None, D_MODEL, tf), lambda i, j: (layer, 0, j)),
                  pl.BlockSpec((None, tf, D_MODEL), lambda i, j: (layer, j, 0))],
        out_specs=pl.BlockSpec((tm, D_MODEL), lambda i, j: (i, 0)),
        out_shape=jax.ShapeDtypeStruct((m, D_MODEL), F32),
        scratch_shapes=[pltpu.VMEM((tm, D_MODEL), BF16)],
        compiler_params=_params("parallel", "arbitrary"),
        name="ffn",
    )(x, gain, wu_bf16, wd_bf16)


def _row_tile(m):
    for tm in (1032, 1024, 688, 512, 344, 256, 128, 64, 32, 16, 8):
        if m % tm == 0:
            return tm
    raise ValueError(f"unsupported row count {m}")


def kernel(x_prompt, x_sample, cache_k, cache_v, state_conv, page_table, meta_tokens, attn_norm, w_in,
           q_norm, k_norm, lambda_q1, lambda_k1, lambda_q2, lambda_k2, attn_subln, dw_w, dw_b,
           conv_ln_g, conv_ln_b, w_out, mlp_norm, w_up, w_down):
    depth = w_in.shape[0]
    b, seq_real, _ = x_prompt.shape
    seq = N_META + seq_real
    db, ds, _ = x_sample.shape

    w_in_b, w_out_b = w_in.astype(BF16), w_out.astype(BF16)
    w_up_b, w_down_b = w_up.astype(BF16), w_down.astype(BF16)
    heads_per_chunk = IN_CHUNK // HEAD_DIM
    blk = jnp.arange(IN_CHUNK) // HEAD_DIM
    gm = jnp.where(blk[:, None] == blk[None, :], 1.0 / HEAD_DIM, 0.0).astype(BF16)

    meta = jnp.broadcast_to(meta_tokens.astype(x_prompt.dtype)[None], (b, N_META, D_MODEL))
    xp = jnp.concatenate([meta, x_prompt], axis=1).reshape(b * seq, D_MODEL)
    xs = x_sample.reshape(db * ds, D_MODEL)
    tm_p, tm_s = _row_tile(b * seq), _row_tile(db * ds)

    kp, vp, cp, ks_, vs_, cs_ = [], [], [], [], [], []
    for l in range(depth):
        lam0 = lambda_init(l)
        lams = [a[l][None, :] for a in (lambda_q1, lambda_k1, lambda_q2, lambda_k2)]
        sub_gain = attn_subln[l][None, :]
        an, mn = attn_norm[l][None, :], mlp_norm[l][None, :]
        qg = jnp.tile(q_norm[l], heads_per_chunk)[None, :]
        kg = jnp.tile(k_norm[l], heads_per_chunk)[None, :]
        cw, cb = dw_w[l], dw_b[l][None, :]
        lg, lb = conv_ln_g[l][None, :], conv_ln_b[l][None, :]

        q, k, v, h = _in_proj(xp, an, w_in_b, l, gm, qg, kg, tm_p)
        oa = _attn_prompt(lams, sub_gain, q, k, v, b, seq, lam0)
        c = _conv_prompt(h, cw, cb, lg, lb, b, seq)
        x1 = _out_proj(xp, oa, c, w_out_b, l, tm_p)
        xp = _ffn(x1, mn, w_up_b, w_down_b, l, tm_p, 512)
        kp.append(k.reshape(b, seq, 2 * N_HEADS, HEAD_DIM))
        vp.append(v.reshape(b, seq, N_HEADS, PAIR_DIM))
        cp.append(h.reshape(b, seq, D_CONV)[:, seq - (CONV_W - 1):])

        q, k, v, h = _in_proj(xs, an, w_in_b, l, gm, qg, kg, tm_s)
        oa = _attn_sample(page_table, lams, sub_gain, q, k, v, cache_k, cache_v, l, lam0)
        ext = jnp.concatenate([state_conv[l], h.reshape(db, ds, D_CONV)], axis=1)
        c = _conv_sample(ext, cw, cb, lg, lb, ds)
        x1 = _out_proj(xs, oa, c, w_out_b, l, tm_s)
        xs = _ffn(x1, mn, w_up_b, w_down_b, l, tm_s, 512)
        ks_.append(k.reshape(db, ds, 2 * N_HEADS, HEAD_DIM))
        vs_.append(v.reshape(db, ds, N_HEADS, PAIR_DIM))
        cs_.append(ext[:, ds:])

    y_prompt = xp.reshape(b, seq, D_MODEL)[:, N_META:]
    y_sample = xs.reshape(db, ds, D_MODEL)
    return (y_prompt, y_sample, jnp.stack(kp), jnp.stack(vp), jnp.stack(cp),
            jnp.stack(ks_), jnp.stack(vs_), jnp.stack(cs_))
```

```python
import functools
import math

import jax
import jax.numpy as jnp
from jax import lax
from jax.experimental import pallas as pl
from jax.experimental.pallas import tpu as pltpu

F32 = jnp.float32
BF16 = jnp.bfloat16

D_MODEL = 2048
N_META = 16
D_ATTN = D_MODEL // 2
N_HEADS = 8
HEAD_DIM = D_ATTN // (2 * N_HEADS)
PAIR_DIM = 2 * HEAD_DIM
D_CONV = D_MODEL - D_ATTN
CONV_W = 31
D_FF = 4 * D_MODEL
D_IN = 3 * D_ATTN + 2 * D_CONV
EPS = 1e-6
NEG_INF = -1e30
QK_SCALE = HEAD_DIM ** -0.5

VMEM_LIMIT_BYTES = 56 * 1024 * 1024

IN_CHUNK = 512
ATTN_TILE = 256
ATTN_CHUNK = 256
ATTN_TAIL_TILE = 128
CONV_ROWS = 48
CONV_HALO = 32
PAGES_PER_STEP = 8


def lambda_init(l):
    return 0.8 - 0.6 * math.exp(-0.3 * l)


def _params(*semantics):
    return pltpu.CompilerParams(dimension_semantics=semantics, vmem_limit_bytes=VMEM_LIMIT_BYTES)


def _rms_rows(x, gain):
    ms = jnp.mean(x * x, axis=-1, keepdims=True)
    return x * lax.rsqrt(ms + EPS) * gain


def _diff_lambda(lq1, lk1, lq2, lk2, lam_init):
    e1 = jnp.exp(jnp.sum(lq1[...] * lk1[...], axis=-1, keepdims=True))
    e2 = jnp.exp(jnp.sum(lq2[...] * lk2[...], axis=-1, keepdims=True))
    return e1 - e2 + lam_init


def _in_proj_kernel(x_ref, g_ref, w_ref, wg_ref, gm_ref, qg_ref, kg_ref,
                    q_ref, k_ref, v_ref, h_ref, xn_ref):
    j = pl.program_id(1)
    n_q = D_ATTN // IN_CHUNK

    @pl.when(j == 0)
    def _():
        xn_ref[...] = _rms_rows(x_ref[...], g_ref[...]).astype(BF16)

    xn = xn_ref[...]
    z = jnp.dot(xn, w_ref[...], preferred_element_type=F32)

    def head_norm(gain):
        ms = jnp.dot((z * z).astype(BF16), gm_ref[...], preferred_element_type=F32)
        return z * lax.rsqrt(ms + EPS) * gain

    @pl.when(j < n_q)
    def _():
        q_ref[...] = (head_norm(qg_ref[...]) * QK_SCALE).astype(BF16)

    @pl.when((j >= n_q) & (j < 2 * n_q))
    def _():
        k_ref[...] = head_norm(kg_ref[...])

    @pl.when((j >= 2 * n_q) & (j < 3 * n_q))
    def _():
        v_ref[...] = z

    @pl.when(j >= 3 * n_q)
    def _():
        gate = jnp.dot(xn, wg_ref[...], preferred_element_type=F32)
        h_ref[...] = z * jax.nn.sigmoid(gate)


def _in_proj(x, gain, w_bf16, layer, gm, qg, kg, tm):
    m = x.shape[0]
    n_q = D_ATTN // IN_CHUNK
    n_steps = 4 * n_q
    gate0 = (3 * D_ATTN + D_CONV) // IN_CHUNK

    def region(r):
        return lambda i, j: (i, jnp.clip(j - r * n_q, 0, n_q - 1))

    return pl.pallas_call(
        _in_proj_kernel,
        grid=(m // tm, n_steps),
        in_specs=[
            pl.BlockSpec((tm, D_MODEL), lambda i, j: (i, 0)),
            pl.BlockSpec((1, D_MODEL), lambda i, j: (0, 0)),
            pl.BlockSpec((None, D_MODEL, IN_CHUNK), lambda i, j: (layer, 0, j)),
            pl.BlockSpec((None, D_MODEL, IN_CHUNK), lambda i, j: (layer, 0, gate0 + jnp.maximum(j - 3 * n_q, 0))),
            pl.BlockSpec((IN_CHUNK, IN_CHUNK), lambda i, j: (0, 0)),
            pl.BlockSpec((1, IN_CHUNK), lambda i, j: (0, 0)),
            pl.BlockSpec((1, IN_CHUNK), lambda i, j: (0, 0)),
        ],
        out_specs=[
            pl.BlockSpec((tm, IN_CHUNK), region(0)),
            pl.BlockSpec((tm, IN_CHUNK), region(1)),
            pl.BlockSpec((tm, IN_CHUNK), region(2)),
            pl.BlockSpec((tm, IN_CHUNK), region(3)),
        ],
        out_shape=[
            jax.ShapeDtypeStruct((m, D_ATTN), BF16),
            jax.ShapeDtypeStruct((m, D_ATTN), F32),
            jax.ShapeDtypeStruct((m, D_ATTN), F32),
            jax.ShapeDtypeStruct((m, D_CONV), F32),
        ],
        scratch_shapes=[pltpu.VMEM((tm, D_MODEL), BF16)],
        compiler_params=_params("parallel", "arbitrary"),
        name="in_proj",
    )(x, gain, w_bf16, w_bf16, gm, qg, kg)


def _sub_ln(o, gain, lam_init):
    ms = jnp.mean(o * o, axis=-1, keepdims=True)
    return o * lax.rsqrt(ms + EPS) * gain * (1.0 - lam_init)


def _scores_t(kc, qq):
    return lax.dot_general(kc, qq, (((1,), (1,)), ((), ())), preferred_element_type=F32)


def _softmax_t(s, m, l):
    m_new = jnp.maximum(m, jnp.max(s, axis=0, keepdims=True))
    alpha = jnp.exp(m - m_new)
    p = jnp.exp(s - m_new)
    return m_new, alpha, p.astype(BF16), alpha * l + jnp.sum(p, axis=0, keepdims=True)


def _attn_prompt_kernel(lq1, lk1, lq2, lk2, sgt_ref, q_ref, k_ref, v_ref, o_ref,
                        kb_ref, vt_ref, *, seq, lam_init):
    lam = _diff_lambda(lq1, lk1, lq2, lk2, lam_init)
    ct, tq = ATTN_CHUNK, ATTN_TILE
    n_chunks = kb_ref.shape[0] // ct
    pad = n_chunks * ct - seq
    kb_ref[pl.ds(0, seq), :] = k_ref[0].astype(BF16)
    kb_ref[pl.ds(seq, pad), :] = jnp.zeros((pad, PAIR_DIM), BF16)
    for c in range(n_chunks):
        rows = min(ct, seq - c * ct)
        vc = v_ref[0, pl.ds(c * ct, rows), :]
        if rows < ct:
            vc = jnp.concatenate([vc, jnp.zeros((ct - rows, PAIR_DIM), F32)], axis=0)
        vt_ref[c] = vc.T.astype(BF16)

    first = lax.broadcasted_iota(jnp.int32, (1, PAIR_DIM), 1) < HEAD_DIM
    n_full, tail = divmod(seq, tq)
    for i in range(pl.cdiv(seq, tq)):
        r0 = i * tq
        n_before = r0 // ct
        if i < n_full:
            t, valid = tq, tq
            qt = q_ref[0, pl.ds(r0, t), :]
        else:
            t, valid = ATTN_TAIL_TILE, tail
            qt = jnp.concatenate([q_ref[0, pl.ds(r0, tail), :], jnp.zeros((t - tail, PAIR_DIM), BF16)], axis=0)
        zero = jnp.zeros_like(qt)
        qq = jnp.concatenate([jnp.where(first, qt, zero), jnp.where(first, zero, qt)], axis=0)
        def scores(c, qq=qq):
            return _scores_t(kb_ref[pl.ds(c * ct, ct), :], qq)

        s = scores(0)
        m, l = jnp.full((1, 2 * t), NEG_INF, F32), jnp.zeros((1, 2 * t), F32)
        acc, p_prev = jnp.zeros((PAIR_DIM, 2 * t), F32), None
        for c in range(n_before + 1):
            s_next = None
            if c == n_before:
                kpos = c * ct + lax.broadcasted_iota(jnp.int32, (ct, 2 * t), 0)
                qpos = r0 + lax.broadcasted_iota(jnp.int32, (ct, 2 * t), 1) % t
                s = jnp.where(kpos <= qpos, s, NEG_INF)
            else:
                s_next = scores(c + 1)
            m, alpha, p, l = _softmax_t(s, m, l)
            if p_prev is not None:
                acc = alpha * (acc + jnp.dot(vt_ref[c - 1], p_prev, preferred_element_type=F32))
            s, p_prev = s_next, p
        acc = acc + jnp.dot(vt_ref[n_before], p_prev, preferred_element_type=F32)
        ot = acc[:, :t] / l[:, :t] - lam * (acc[:, t:] / l[:, t:])
        ms = jnp.mean(ot * ot, axis=0, keepdims=True)
        ot = ot * lax.rsqrt(ms + EPS) * sgt_ref[...] * (1.0 - lam_init)
        o_ref[0, pl.ds(r0, valid), :] = ot.T[:valid].astype(BF16)


def _attn_prompt(lams, sub_gain, q, k, v, batch, seq, lam_init):
    q3 = q.reshape(batch, seq, D_ATTN)
    k3 = k.reshape(batch, seq, D_ATTN)
    v3 = v.reshape(batch, seq, D_ATTN)
    n_chunks = pl.cdiv(seq, ATTN_CHUNK)
    assert seq % ATTN_TILE <= ATTN_TAIL_TILE and ATTN_CHUNK % ATTN_TILE == 0
    vec = pl.BlockSpec((1, HEAD_DIM), lambda b, h: (0, 0))
    blk = pl.BlockSpec((1, seq, PAIR_DIM), lambda b, h: (b, 0, h))
    out = pl.pallas_call(
        functools.partial(_attn_prompt_kernel, seq=seq, lam_init=lam_init),
        grid=(batch, N_HEADS),
        in_specs=[vec, vec, vec, vec, pl.BlockSpec((PAIR_DIM, 1), lambda b, h: (0, 0)), blk, blk, blk],
        out_specs=blk,
        out_shape=jax.ShapeDtypeStruct((batch, seq, D_ATTN), BF16),
        scratch_shapes=[pltpu.VMEM((n_chunks * ATTN_CHUNK, PAIR_DIM), BF16),
                        pltpu.VMEM((n_chunks, PAIR_DIM, ATTN_CHUNK), BF16)],
        compiler_params=_params("parallel", "parallel"),
        name="attn_prompt",
    )(*lams, sub_gain.reshape(PAIR_DIM, 1), q3, k3, v3)
    return out.reshape(batch * seq, D_ATTN)


SAMPLE_ROWS = 8


def _attn_sample_kernel(pt_ref, lq1, lk1, lq2, lk2, sg_ref, qa_ref, qb_ref, kn_ref, vn_ref, *rest,
                        n_q, page, lam_init):
    del pt_ref
    npg = PAGES_PER_STEP
    k_refs, v_refs = rest[:npg], rest[npg:2 * npg]
    o_ref, m_ref, l_ref, acc_ref = rest[2 * npg:]
    s_idx = pl.program_id(1)
    rows = N_HEADS * SAMPLE_ROWS
    half = SAMPLE_ROWS // 2

    @pl.when(s_idx == 0)
    def _():
        m_ref[...] = jnp.full(m_ref.shape, NEG_INF, F32)
        l_ref[...] = jnp.zeros(l_ref.shape, F32)
        acc_ref[...] = jnp.zeros(acc_ref.shape, F32)

    qa = [qa_ref[0, h].astype(BF16) for h in range(N_HEADS)]
    qb = [qb_ref[0, h].astype(BF16) for h in range(N_HEADS)]

    def scores(k_head):
        blocks = []
        for h in range(N_HEADS):
            s1 = jnp.dot(qa[h], k_head(2 * h).astype(BF16), preferred_element_type=F32)
            s2 = jnp.dot(qb[h], k_head(2 * h + 1).astype(BF16), preferred_element_type=F32)
            blocks.append(s1 + s2)
        return jnp.concatenate(blocks, axis=0)

    def update(s, v_heads):
        m = m_ref[...]
        m_new = jnp.maximum(m, jnp.max(s, axis=-1, keepdims=True))
        alpha = jnp.exp(m - m_new)
        p = jnp.exp(s - m_new)
        l_ref[...] = alpha * l_ref[...] + jnp.sum(p, axis=-1, keepdims=True)
        pv = []
        for h in range(N_HEADS):
            ph = p[h * SAMPLE_ROWS:(h + 1) * SAMPLE_ROWS]
            acc = None
            for i, v_head in enumerate(v_heads):
                d = jnp.dot(ph[:, i * page:(i + 1) * page].astype(BF16), v_head(h).astype(BF16),
                            preferred_element_type=F32)
                acc = d if acc is None else acc + d
            pv.append(acc)
        acc_ref[...] = alpha * acc_ref[...] + jnp.concatenate(pv, axis=0)
        m_ref[...] = m_new

    def k_slice(ref, lead):
        return lambda h16: ref[(*lead, h16)]

    def v_slice(ref, lead):

        return lambda h: ref[(*lead, pl.ds(h, page, stride=N_HEADS), slice(None))]

    s = jnp.concatenate([scores(k_slice(kr, (0, 0))) for kr in k_refs], axis=1)
    update(s, [v_slice(vr, (0, 0)) for vr in v_refs])

    @pl.when(s_idx == pl.num_programs(1) - 1)
    def _():
        qi = lax.broadcasted_iota(jnp.int32, (rows, page), 0) % half
        kj = lax.broadcasted_iota(jnp.int32, (rows, page), 1)
        s_new = scores(k_slice(kn_ref, (0,)))
        update(jnp.where(kj <= qi, s_new, NEG_INF), [v_slice(vn_ref, (0,))])

        f = acc_ref[...] / l_ref[...]
        lam = _diff_lambda(lq1, lk1, lq2, lk2, lam_init)
        for h in range(N_HEADS):
            r0 = h * SAMPLE_ROWS
            o = f[r0:r0 + SAMPLE_ROWS] - lam * pltpu.roll(f[r0:r0 + SAMPLE_ROWS], half, 0)
            o_ref[0, h] = _sub_ln(o, sg_ref[...], lam_init)


def _attn_sample(page_table, lams, sub_gain, q, k_new, v_new, cache_k, cache_v, layer, lam_init):
    db, n_pages = page_table.shape
    n_q = q.shape[0] // db
    page = cache_k.shape[2]
    npg = PAGES_PER_STEP
    rows = N_HEADS * SAMPLE_ROWS
    half = SAMPLE_ROWS // 2
    assert n_q <= half

    q5 = q.astype(F32).reshape(db, n_q, N_HEADS, 2, HEAD_DIM).transpose(0, 2, 3, 1, 4)
    q5 = jnp.pad(q5, ((0, 0), (0, 0), (0, 0), (0, half - n_q), (0, 0)))
    zeros = jnp.zeros_like(q5[:, :, 0])
    qa = jnp.concatenate([q5[:, :, 0], zeros], axis=2)
    qb = jnp.concatenate([zeros, q5[:, :, 1]], axis=2)
    v_rows = page * N_HEADS
    kn = jnp.pad(k_new.reshape(db, n_q, 2 * N_HEADS, HEAD_DIM).transpose(0, 2, 3, 1),
                 ((0, 0), (0, 0), (0, 0), (0, page - n_q)))
    vn = jnp.pad(v_new.reshape(db, n_q * N_HEADS, PAIR_DIM), ((0, 0), (0, v_rows - n_q * N_HEADS), (0, 0)))
    ck = cache_k.transpose(0, 1, 3, 4, 2)
    cv = cache_v.reshape(cache_v.shape[0], cache_v.shape[1], v_rows, PAIR_DIM)

    vec = pl.BlockSpec((1, HEAD_DIM), lambda b, s, pt: (0, 0))
    qspec = pl.BlockSpec((1, N_HEADS, SAMPLE_ROWS, HEAD_DIM), lambda b, s, pt: (b, 0, 0, 0))
    k_page = (2 * N_HEADS, HEAD_DIM, page)

    def k_spec(p):
        return pl.BlockSpec((1, 1) + k_page, lambda b, s, pt: (layer, pt[b, s * npg + p], 0, 0, 0))

    def v_spec(p):
        return pl.BlockSpec((1, 1, v_rows, PAIR_DIM), lambda b, s, pt: (layer, pt[b, s * npg + p], 0, 0))

    grid_spec = pltpu.PrefetchScalarGridSpec(
        num_scalar_prefetch=1,
        grid=(db, n_pages // npg),
        in_specs=[vec, vec, vec, vec, pl.BlockSpec((1, PAIR_DIM), lambda b, s, pt: (0, 0)), qspec, qspec,
                  pl.BlockSpec((1,) + k_page, lambda b, s, pt: (b, 0, 0, 0)),
                  pl.BlockSpec((1, v_rows, PAIR_DIM), lambda b, s, pt: (b, 0, 0))]
                 + [k_spec(p) for p in range(npg)] + [v_spec(p) for p in range(npg)],
        out_specs=pl.BlockSpec((1, N_HEADS, SAMPLE_ROWS, PAIR_DIM), lambda b, s, pt: (b, 0, 0, 0)),
        scratch_shapes=[pltpu.VMEM((rows, 1), F32), pltpu.VMEM((rows, 1), F32),
                        pltpu.VMEM((rows, PAIR_DIM), F32)],
    )
    out = pl.pallas_call(
        functools.partial(_attn_sample_kernel, n_q=n_q, page=page, lam_init=lam_init),
        grid_spec=grid_spec,
        out_shape=jax.ShapeDtypeStruct((db, N_HEADS, SAMPLE_ROWS, PAIR_DIM), F32),
        compiler_params=_params("parallel", "arbitrary"),
        name="attn_sample",
    )(page_table, *lams, sub_gain, qa, qb, kn, vn, *([ck] * npg), *([cv] * npg))
    return out[:, :, :n_q].transpose(0, 2, 1, 3).reshape(db * n_q, D_ATTN).astype(BF16)


def _ln_silu(cf, g, b):
    mu = jnp.mean(cf, axis=-1, keepdims=True)
    d = cf - mu
    var = jnp.mean(d * d, axis=-1, keepdims=True)
    cn = d * lax.rsqrt(var + EPS) * g + b
    return cn * jax.nn.sigmoid(cn)


def _conv_prompt_kernel(h_ref, w_ref, b_ref, g_ref, bb_ref, o_ref, ext_ref, sh_ref, *, seq):
    lanes = 128
    n_sh = sh_ref.shape[1]
    ext_ref[pl.ds(0, CONV_HALO), :] = jnp.zeros((CONV_HALO, D_CONV), F32)
    ext_ref[pl.ds(CONV_HALO, seq), :] = h_ref[0]
    ext_ref[pl.ds(CONV_HALO + seq, 8), :] = jnp.zeros((8, D_CONV), F32)
    shift0 = CONV_HALO - (CONV_W - 1)
    n_tiles = seq // CONV_ROWS

    for c in range(D_CONV // lanes):
        cols = pl.ds(c * lanes, lanes)
        for r in range(8):
            sh_ref[r] = ext_ref[pl.ds(r, n_sh), cols]

        def tile(it, carry, cols=cols):
            t0 = pl.multiple_of(it * CONV_ROWS, 8)
            acc = jnp.zeros((CONV_ROWS, lanes), F32)
            for tap in range(CONV_W):
                a, r = divmod(tap + shift0, 8)
                acc = acc + sh_ref[r, pl.ds(t0 + 8 * a, CONV_ROWS), :] * w_ref[pl.ds(tap, 1), cols]
            ext_ref[pl.ds(CONV_HALO + t0, CONV_ROWS), cols] = acc
            return carry

        lax.fori_loop(0, n_tiles, tile, 0, unroll=2)

    def ln_tile(it, carry):
        t0 = pl.multiple_of(it * CONV_ROWS, 8)
        cf = ext_ref[pl.ds(CONV_HALO + t0, CONV_ROWS), :] + b_ref[...]
        o_ref[0, pl.ds(t0, CONV_ROWS), :] = _ln_silu(cf, g_ref[...], bb_ref[...]).astype(BF16)
        return carry

    lax.fori_loop(0, n_tiles, ln_tile, 0, unroll=4)


def _conv_prompt(h, dw_w, dw_b, ln_g, ln_b, batch, seq):
    h3 = h.reshape(batch, seq, D_CONV)
    row = pl.BlockSpec((1, D_CONV), lambda b: (0, 0))
    out = pl.pallas_call(
        functools.partial(_conv_prompt_kernel, seq=seq),
        grid=(batch,),
        in_specs=[pl.BlockSpec((1, seq, D_CONV), lambda b: (b, 0, 0)),
                  pl.BlockSpec((CONV_W, D_CONV), lambda b: (0, 0)), row, row, row],
        out_specs=pl.BlockSpec((1, seq, D_CONV), lambda b: (b, 0, 0)),
        out_shape=jax.ShapeDtypeStruct((batch, seq, D_CONV), BF16),
        scratch_shapes=[pltpu.VMEM((CONV_HALO + seq + 8, D_CONV), F32),
                        pltpu.VMEM((8, CONV_HALO + seq, 128), F32)],
        compiler_params=_params("parallel"),
        name="conv_prompt",
    )(h3, dw_w, dw_b, ln_g, ln_b)
    return out.reshape(batch * seq, D_CONV)


def _conv_sample_kernel(ext_ref, w_ref, b_ref, g_ref, bb_ref, o_ref, *, n_q):
    w = w_ref[...]
    for t in range(n_q):
        cf = jnp.sum(ext_ref[0, pl.ds(t, CONV_W), :] * w, axis=0, keepdims=True) + b_ref[...]
        o_ref[0, pl.ds(t, 1), :] = _ln_silu(cf, g_ref[...], bb_ref[...])


def _conv_sample(ext, dw_w, dw_b, ln_g, ln_b, n_q):
    db, rows, _ = ext.shape
    row = pl.BlockSpec((1, D_CONV), lambda b: (0, 0))
    out = pl.pallas_call(
        functools.partial(_conv_sample_kernel, n_q=n_q),
        grid=(db,),
        in_specs=[pl.BlockSpec((1, rows, D_CONV), lambda b: (b, 0, 0)),
                  pl.BlockSpec((CONV_W, D_CONV), lambda b: (0, 0)), row, row, row],
        out_specs=pl.BlockSpec((1, n_q, D_CONV), lambda b: (b, 0, 0)),
        out_shape=jax.ShapeDtypeStruct((db, n_q, D_CONV), F32),
        compiler_params=_params("parallel"),
        name="conv_sample",
    )(ext, dw_w, dw_b, ln_g, ln_b)
    return out.reshape(db * n_q, D_CONV).astype(BF16)


def _out_proj_kernel(x_ref, oa_ref, c_ref, wa_ref, wc_ref, o_ref):
    o_ref[...] = (x_ref[...]
                  + jnp.dot(oa_ref[...], wa_ref[...], preferred_element_type=F32)
                  + jnp.dot(c_ref[...], wc_ref[...], preferred_element_type=F32))


def _out_proj(x, oa, c, w_bf16, layer, tm):
    m = x.shape[0]
    tn = D_MODEL // 2
    return pl.pallas_call(
        _out_proj_kernel,
        grid=(m // tm, D_MODEL // tn),
        in_specs=[pl.BlockSpec((tm, tn), lambda i, j: (i, j)),
                  pl.BlockSpec((tm, D_ATTN), lambda i, j: (i, 0)),
                  pl.BlockSpec((tm, D_CONV), lambda i, j: (i, 0)),
                  pl.BlockSpec((None, D_ATTN, tn), lambda i, j: (layer, 0, j)),
                  pl.BlockSpec((None, D_CONV, tn), lambda i, j: (layer, D_ATTN // D_CONV, j))],
        out_specs=pl.BlockSpec((tm, tn), lambda i, j: (i, j)),
        out_shape=jax.ShapeDtypeStruct((m, D_MODEL), F32),
        compiler_params=_params("parallel", "arbitrary"),
        name="out_proj",
    )(x, oa, c, w_bf16, w_bf16)


def _ffn_kernel(x_ref, g_ref, wu_ref, wd_ref, o_ref, xn_ref):
    @pl.when(pl.program_id(1) == 0)
    def _():
        x = x_ref[...]
        xn_ref[...] = _rms_rows(x, g_ref[...]).astype(BF16)
        o_ref[...] = x

    h = jnp.dot(xn_ref[...], wu_ref[...], preferred_element_type=F32)
    a = jnp.square(jnp.maximum(h, 0.0)).astype(BF16)
    o_ref[...] += jnp.dot(a, wd_ref[...], preferred_element_type=F32)


def _ffn(x, gain, wu_bf16, wd_bf16, layer, tm, tf):
    m = x.shape[0]
    return pl.pallas_call(
        _ffn_kernel,
        grid=(m // tm, D_FF // tf),
        in_specs=[pl.BlockSpec((tm, D_MODEL), lambda i, j: (i, 0)),
                  pl.BlockSpec((1, D_MODEL), lambda i, j: (0, 0)),
                  pl.BlockSpec((None, D_MODEL, tf), lambda i, j: (layer, 0, j)),
                  pl.BlockSpec((None, tf, D_MODEL), lambda i, j: (layer, j, 0))],
        out_specs=pl.BlockSpec((tm, D_MODEL), lambda i, j: (i, 0)),
        out_shape=jax.ShapeDtypeStruct((m, D_MODEL), F32),
        scratch_shapes=[pltpu.VMEM((tm, D_MODEL), BF16)],
        compiler_params=_params("parallel", "arbitrary"),
        name="ffn",
    )(x, gain, wu_bf16, wd_bf16)


def _row_tile(m):
    for tm in (1032, 1024, 688, 512, 344, 256, 128, 64, 32, 16, 8):
        if m % tm == 0:
            return tm
    raise ValueError(f"unsupported row count {m}")


def kernel(x_prompt, x_sample, cache_k, cache_v, state_conv, page_table, meta_tokens, attn_norm, w_in,
           q_norm, k_norm, lambda_q1, lambda_k1, lambda_q2, lambda_k2, attn_subln, dw_w, dw_b,
           conv_ln_g, conv_ln_b, w_out, mlp_norm, w_up, w_down):
    depth = w_in.shape[0]
    b, seq_real, _ = x_prompt.shape
    seq = N_META + seq_real
    db, ds, _ = x_sample.shape

    w_in_b, w_out_b = w_in.astype(BF16), w_out.astype(BF16)
    w_up_b, w_down_b = w_up.astype(BF16), w_down.astype(BF16)
    heads_per_chunk = IN_CHUNK // HEAD_DIM
    blk = jnp.arange(IN_CHUNK) // HEAD_DIM
    gm = jnp.where(blk[:, None] == blk[None, :], 1.0 / HEAD_DIM, 0.0).astype(BF16)

    meta = jnp.broadcast_to(meta_tokens.astype(x_prompt.dtype)[None], (b, N_META, D_MODEL))
    xp = jnp.concatenate([meta, x_prompt], axis=1).reshape(b * seq, D_MODEL)
    xs = x_sample.reshape(db * ds, D_MODEL)
    tm_p, tm_s = _row_tile(b * seq), _row_tile(db * ds)

    kp, vp, cp, ks_, vs_, cs_ = [], [], [], [], [], []
    for l in range(depth):
        lam0 = lambda_init(l)
        lams = [a[l][None, :] for a in (lambda_q1, lambda_k1, lambda_q2, lambda_k2)]
        sub_gain = attn_subln[l][None, :]
        an, mn = attn_norm[l][None, :], mlp_norm[l][None, :]
        qg = jnp.tile(q_norm[l], heads_per_chunk)[None, :]
        kg = jnp.tile(k_norm[l], heads_per_chunk)[None, :]
        cw, cb = dw_w[l], dw_b[l][None, :]
        lg, lb = conv_ln_g[l][None, :], conv_ln_b[l][None, :]

        q, k, v, h = _in_proj(xp, an, w_in_b, l, gm, qg, kg, tm_p)
        oa = _attn_prompt(lams, sub_gain, q, k, v, b, seq, lam0)
        c = _conv_prompt(h, cw, cb, lg, lb, b, seq)
        x1 = _out_proj(xp, oa, c, w_out_b, l, tm_p)
        xp = _ffn(x1, mn, w_up_b, w_down_b, l, tm_p, 512)
        kp.append(k.reshape(b, seq, 2 * N_HEADS, HEAD_DIM))
        vp.append(v.reshape(b, seq, N_HEADS, PAIR_DIM))
        cp.append(h.reshape(b, seq, D_CONV)[:, seq - (CONV_W - 1):])

        q, k, v, h = _in_proj(xs, an, w_in_b, l, gm, qg, kg, tm_s)
        oa = _attn_sample(page_table, lams, sub_gain, q, k, v, cache_k, cache_v, l, lam0)
        ext = jnp.concatenate([state_conv[l], h.reshape(db, ds, D_CONV)], axis=1)
        c = _conv_sample(ext, cw, cb, lg, lb, ds)
        x1 = _out_proj(xs, oa, c, w_out_b, l, tm_s)
        xs = _ffn(x1, mn, w_up_b, w_down_b, l, tm_s, 512)
        ks_.append(k.reshape(db, ds, 2 * N_HEADS, HEAD_DIM))
        vs_.append(v.reshape(db, ds, N_HEADS, PAIR_DIM))
        cs_.append(ext[:, ds:])

    y_prompt = xp.reshape(b, seq, D_MODEL)[:, N_META:]
    y_sample = xs.reshape(db, ds, D_MODEL)
    return (y_prompt, y_sample, jnp.stack(kp), jnp.stack(vp), jnp.stack(cp),
            jnp.stack(ks_), jnp.stack(vs_), jnp.stack(cs_))
```

```python
import functools
import math

import jax
import jax.numpy as jnp
from jax import lax
from jax.experimental import pallas as pl
from jax.experimental.pallas import tpu as pltpu

F32 = jnp.float32
BF16 = jnp.bfloat16

D_MODEL = 2048
N_META = 16
D_ATTN = D_MODEL // 2
N_HEADS = 8
HEAD_DIM = D_ATTN // (2 * N_HEADS)
PAIR_DIM = 2 * HEAD_DIM
D_CONV = D_MODEL - D_ATTN
CONV_W = 31
D_FF = 4 * D_MODEL
D_IN = 3 * D_ATTN + 2 * D_CONV
EPS = 1e-6
NEG_INF = -1e30
QK_SCALE = HEAD_DIM ** -0.5 * math.log2(math.e)

VMEM_LIMIT_BYTES = 56 * 1024 * 1024

IN_CHUNK = 512
FFN_CHUNK = 512
ATTN_TILE = 256
ATTN_CHUNK = 256
ATTN_TAIL_TILE = 128
CONV_ROWS = 48
CONV_HALO = 32
PAGES_PER_STEP = 8


def lambda_init(l):
    return 0.8 - 0.6 * math.exp(-0.3 * l)


def _params(*semantics):
    return pltpu.CompilerParams(dimension_semantics=semantics, vmem_limit_bytes=VMEM_LIMIT_BYTES)


def _rms_rows(x, gain):
    ms = jnp.mean(x * x, axis=-1, keepdims=True)
    return x * lax.rsqrt(ms + EPS) * gain


def _diff_lambda(lq1, lk1, lq2, lk2, lam_init):
    e1 = jnp.exp(jnp.sum(lq1[...] * lk1[...], axis=-1, keepdims=True))
    e2 = jnp.exp(jnp.sum(lq2[...] * lk2[...], axis=-1, keepdims=True))
    return e1 - e2 + lam_init


def _in_proj_kernel(x_ref, g_ref, w_ref, wg_ref, gm_ref, qg_ref, kg_ref,
                    q_ref, k_ref, v_ref, h_ref, xn_ref):
    j = pl.program_id(1)
    n_q = D_ATTN // IN_CHUNK

    @pl.when(j == 0)
    def _():
        xn_ref[...] = _rms_rows(x_ref[...], g_ref[...]).astype(BF16)

    xn = xn_ref[...]
    z = jnp.dot(xn, w_ref[...], preferred_element_type=F32)

    def head_norm(gain):
        ms = jnp.dot((z * z).astype(BF16), gm_ref[...], preferred_element_type=F32)
        return z * lax.rsqrt(ms + EPS) * gain

    @pl.when(j < n_q)
    def _():
        q_ref[...] = (head_norm(qg_ref[...]) * QK_SCALE).astype(BF16)

    @pl.when((j >= n_q) & (j < 2 * n_q))
    def _():
        k_ref[...] = head_norm(kg_ref[...])

    @pl.when((j >= 2 * n_q) & (j < 3 * n_q))
    def _():
        v_ref[...] = z

    @pl.when(j >= 3 * n_q)
    def _():
        gate = jnp.dot(xn, wg_ref[...], preferred_element_type=F32)
        h_ref[...] = z * jax.nn.sigmoid(gate)


def _in_proj(x, gain, w_bf16, layer, gm, qg, kg, tm):
    m = x.shape[0]
    n_q = D_ATTN // IN_CHUNK
    n_steps = 4 * n_q
    gate0 = (3 * D_ATTN + D_CONV) // IN_CHUNK

    def region(r):
        return lambda i, j: (i, jnp.clip(j - r * n_q, 0, n_q - 1))

    return pl.pallas_call(
        _in_proj_kernel,
        grid=(m // tm, n_steps),
        in_specs=[
            pl.BlockSpec((tm, D_MODEL), lambda i, j: (i, 0)),
            pl.BlockSpec((1, D_MODEL), lambda i, j: (0, 0)),
            pl.BlockSpec((None, D_MODEL, IN_CHUNK), lambda i, j: (layer, 0, j)),
            pl.BlockSpec((None, D_MODEL, IN_CHUNK), lambda i, j: (layer, 0, gate0 + jnp.maximum(j - 3 * n_q, 0))),
            pl.BlockSpec((IN_CHUNK, IN_CHUNK), lambda i, j: (0, 0)),
            pl.BlockSpec((1, IN_CHUNK), lambda i, j: (0, 0)),
            pl.BlockSpec((1, IN_CHUNK), lambda i, j: (0, 0)),
        ],
        out_specs=[
            pl.BlockSpec((tm, IN_CHUNK), region(0)),
            pl.BlockSpec((tm, IN_CHUNK), region(1)),
            pl.BlockSpec((tm, IN_CHUNK), region(2)),
            pl.BlockSpec((tm, IN_CHUNK), region(3)),
        ],
        out_shape=[
            jax.ShapeDtypeStruct((m, D_ATTN), BF16),
            jax.ShapeDtypeStruct((m, D_ATTN), F32),
            jax.ShapeDtypeStruct((m, D_ATTN), F32),
            jax.ShapeDtypeStruct((m, D_CONV), F32),
        ],
        scratch_shapes=[pltpu.VMEM((tm, D_MODEL), BF16)],
        compiler_params=_params("parallel", "arbitrary"),
        name="in_proj",
    )(x, gain, w_bf16, w_bf16, gm, qg, kg)


def _sub_ln(o, gain, lam_init):
    ms = jnp.mean(o * o, axis=-1, keepdims=True)
    return o * lax.rsqrt(ms + EPS) * gain * (1.0 - lam_init)


def _scores_t(kc, qq):
    return lax.dot_general(kc, qq, (((1,), (1,)), ((), ())), preferred_element_type=F32)


def _softmax_t(s, m, l):
    m_new = jnp.maximum(m, jnp.max(s, axis=0, keepdims=True))
    alpha = jnp.exp2(m - m_new)
    p = jnp.exp2(s - m_new)
    return m_new, alpha, p.astype(BF16), alpha * l + jnp.sum(p, axis=0, keepdims=True)


def _attn_prompt_kernel(lq1, lk1, lq2, lk2, sgt_ref, q_ref, k_ref, v_ref, o_ref, kt_ref,
                        kb_ref, vt_ref, *, seq, lam_init):
    lam = _diff_lambda(lq1, lk1, lq2, lk2, lam_init)
    ct, tq = ATTN_CHUNK, ATTN_TILE
    n_chunks = kb_ref.shape[0] // ct
    pad = n_chunks * ct - seq
    kb_ref[pl.ds(0, seq), :] = k_ref[0].astype(BF16)
    kb_ref[pl.ds(seq, pad), :] = jnp.zeros((pad, PAIR_DIM), BF16)
    for c in range(n_chunks):
        rows = min(ct, seq - c * ct)
        vc, kc = v_ref[0, pl.ds(c * ct, rows), :], k_ref[0, pl.ds(c * ct, rows), :]
        if rows < ct:
            fill = jnp.zeros((ct - rows, PAIR_DIM), F32)
            vc, kc = jnp.concatenate([vc, fill], axis=0), jnp.concatenate([kc, fill], axis=0)
        vt_ref[c] = vc.T.astype(BF16)
        kt_ref[0, :, pl.ds(c * ct, rows)] = kc.T[:, :rows]

    first = lax.broadcasted_iota(jnp.int32, (1, PAIR_DIM), 1) < HEAD_DIM
    n_full, tail = divmod(seq, tq)
    for i in range(pl.cdiv(seq, tq)):
        r0 = i * tq
        n_before = r0 // ct
        if i < n_full:
            t, valid = tq, tq
            qt = q_ref[0, pl.ds(r0, t), :]
        else:
            t, valid = ATTN_TAIL_TILE, tail
            qt = jnp.concatenate([q_ref[0, pl.ds(r0, tail), :], jnp.zeros((t - tail, PAIR_DIM), BF16)], axis=0)
        zero = jnp.zeros_like(qt)
        qq = jnp.concatenate([jnp.where(first, qt, zero), jnp.where(first, zero, qt)], axis=0)
        def scores(c, qq=qq):
            return _scores_t(kb_ref[pl.ds(c * ct, ct), :], qq)

        s = scores(0)
        m, l = jnp.full((1, 2 * t), NEG_INF, F32), jnp.zeros((1, 2 * t), F32)
        acc, p_prev = jnp.zeros((PAIR_DIM, 2 * t), F32), None
        for c in range(n_before + 1):
            s_next = None
            if c == n_before:
                kpos = c * ct + lax.broadcasted_iota(jnp.int32, (ct, 2 * t), 0)
                qpos = r0 + lax.broadcasted_iota(jnp.int32, (ct, 2 * t), 1) % t
                s = jnp.where(kpos <= qpos, s, NEG_INF)
            else:
                s_next = scores(c + 1)
            m, alpha, p, l = _softmax_t(s, m, l)
            if p_prev is not None:
                acc = alpha * (acc + jnp.dot(vt_ref[c - 1], p_prev, preferred_element_type=F32))
            s, p_prev = s_next, p
        acc = acc + jnp.dot(vt_ref[n_before], p_prev, preferred_element_type=F32)
        ot = acc[:, :t] / l[:, :t] - lam * (acc[:, t:] / l[:, t:])
        ms = jnp.mean(ot * ot, axis=0, keepdims=True)
        ot = ot * lax.rsqrt(ms + EPS) * sgt_ref[...] * (1.0 - lam_init)
        o_ref[0, pl.ds(r0, valid), :] = ot.T[:valid].astype(BF16)


def _attn_prompt(lams, sub_gain, q, k, v, batch, seq, lam_init):
    q3 = q.reshape(batch, seq, D_ATTN)
    k3 = k.reshape(batch, seq, D_ATTN)
    v3 = v.reshape(batch, seq, D_ATTN)
    n_chunks = pl.cdiv(seq, ATTN_CHUNK)
    assert seq % ATTN_TILE <= ATTN_TAIL_TILE and ATTN_CHUNK % ATTN_TILE == 0
    vec = pl.BlockSpec((1, HEAD_DIM), lambda b, h: (0, 0))
    blk = pl.BlockSpec((1, seq, PAIR_DIM), lambda b, h: (b, 0, h))
    out, kt = pl.pallas_call(
        functools.partial(_attn_prompt_kernel, seq=seq, lam_init=lam_init),
        grid=(batch, N_HEADS),
        in_specs=[vec, vec, vec, vec, pl.BlockSpec((PAIR_DIM, 1), lambda b, h: (0, 0)), blk, blk, blk],
        out_specs=[blk, pl.BlockSpec((1, PAIR_DIM, seq), lambda b, h: (b, h, 0))],
        out_shape=[jax.ShapeDtypeStruct((batch, seq, D_ATTN), BF16),
                   jax.ShapeDtypeStruct((batch, D_ATTN, seq), F32)],
        scratch_shapes=[pltpu.VMEM((n_chunks * ATTN_CHUNK, PAIR_DIM), BF16),
                        pltpu.VMEM((n_chunks, PAIR_DIM, ATTN_CHUNK), BF16)],
        compiler_params=_params("parallel", "parallel"),
        name="attn_prompt",
    )(*lams, sub_gain.reshape(PAIR_DIM, 1), q3, k3, v3)
    return out.reshape(batch * seq, D_ATTN), kt.reshape(batch, 2 * N_HEADS, HEAD_DIM, seq)


SAMPLE_ROWS = 8


def _attn_sample_kernel(pt_ref, lq1, lk1, lq2, lk2, sg_ref, qa_ref, qb_ref, kn_ref, vn_ref, *rest,
                        n_q, page, lam_init):
    del pt_ref
    npg = PAGES_PER_STEP
    k_refs, v_refs = rest[:npg], rest[npg:2 * npg]
    o_ref, m_ref, l_ref, acc_ref = rest[2 * npg:]
    s_idx = pl.program_id(1)
    rows = N_HEADS * SAMPLE_ROWS
    half = SAMPLE_ROWS // 2

    @pl.when(s_idx == 0)
    def _():
        m_ref[...] = jnp.full(m_ref.shape, NEG_INF, F32)
        l_ref[...] = jnp.zeros(l_ref.shape, F32)
        acc_ref[...] = jnp.zeros(acc_ref.shape, F32)

    qa = [qa_ref[0, h].astype(BF16) for h in range(N_HEADS)]
    qb = [qb_ref[0, h].astype(BF16) for h in range(N_HEADS)]

    def scores(k_head):
        blocks = []
        for h in range(N_HEADS):
            s1 = jnp.dot(qa[h], k_head(2 * h).astype(BF16), preferred_element_type=F32)
            s2 = jnp.dot(qb[h], k_head(2 * h + 1).astype(BF16), preferred_element_type=F32)
            blocks.append(s1 + s2)
        return jnp.concatenate(blocks, axis=0)

    def update(s, v_heads):
        m = m_ref[...]
        m_new = jnp.maximum(m, jnp.max(s, axis=-1, keepdims=True))
        alpha = jnp.exp2(m - m_new)
        p = jnp.exp2(s - m_new)
        l_ref[...] = alpha * l_ref[...] + jnp.sum(p, axis=-1, keepdims=True)
        pv = []
        for h in range(N_HEADS):
            ph = p[h * SAMPLE_ROWS:(h + 1) * SAMPLE_ROWS]
            acc = None
            for i, v_head in enumerate(v_heads):
                d = jnp.dot(ph[:, i * page:(i + 1) * page].astype(BF16), v_head(h).astype(BF16),
                            preferred_element_type=F32)
                acc = d if acc is None else acc + d
            pv.append(acc)
        acc_ref[...] = alpha * acc_ref[...] + jnp.concatenate(pv, axis=0)
        m_ref[...] = m_new

    def k_slice(ref, lead):
        return lambda h16: ref[(*lead, h16)]

    def v_slice(ref, lead):

        return lambda h: ref[(*lead, pl.ds(h, page, stride=N_HEADS), slice(None))]

    s = jnp.concatenate([scores(k_slice(kr, (0, 0))) for kr in k_refs], axis=1)
    update(s, [v_slice(vr, (0, 0)) for vr in v_refs])

    @pl.when(s_idx == pl.num_programs(1) - 1)
    def _():
        qi = lax.broadcasted_iota(jnp.int32, (rows, page), 0) % half
        kj = lax.broadcasted_iota(jnp.int32, (rows, page), 1)
        s_new = scores(k_slice(kn_ref, (0,)))
        update(jnp.where(kj <= qi, s_new, NEG_INF), [v_slice(vn_ref, (0,))])

        f = acc_ref[...] / l_ref[...]
        lam = _diff_lambda(lq1, lk1, lq2, lk2, lam_init)
        for h in range(N_HEADS):
            r0 = h * SAMPLE_ROWS
            o = f[r0:r0 + SAMPLE_ROWS] - lam * pltpu.roll(f[r0:r0 + SAMPLE_ROWS], half, 0)
            o_ref[0, h] = _sub_ln(o, sg_ref[...], lam_init)


def _attn_sample(page_table, lams, sub_gain, q, k_new, v_new, cache_k, cache_v, layer, lam_init):
    db, n_pages = page_table.shape
    n_q = q.shape[0] // db
    page = cache_k.shape[2]
    npg = PAGES_PER_STEP
    rows = N_HEADS * SAMPLE_ROWS
    half = SAMPLE_ROWS // 2
    assert n_q <= half

    q5 = q.astype(F32).reshape(db, n_q, N_HEADS, 2, HEAD_DIM).transpose(0, 2, 3, 1, 4)
    q5 = jnp.pad(q5, ((0, 0), (0, 0), (0, 0), (0, half - n_q), (0, 0)))
    zeros = jnp.zeros_like(q5[:, :, 0])
    qa = jnp.concatenate([q5[:, :, 0], zeros], axis=2)
    qb = jnp.concatenate([zeros, q5[:, :, 1]], axis=2)
    v_rows = page * N_HEADS
    kn = jnp.pad(k_new.reshape(db, n_q, 2 * N_HEADS, HEAD_DIM).transpose(0, 2, 3, 1),
                 ((0, 0), (0, 0), (0, 0), (0, page - n_q)))
    vn = jnp.pad(v_new.reshape(db, n_q * N_HEADS, PAIR_DIM), ((0, 0), (0, v_rows - n_q * N_HEADS), (0, 0)))
    ck = cache_k.transpose(0, 1, 3, 4, 2)
    cv = cache_v.reshape(cache_v.shape[0], cache_v.shape[1], v_rows, PAIR_DIM)

    vec = pl.BlockSpec((1, HEAD_DIM), lambda b, s, pt: (0, 0))
    qspec = pl.BlockSpec((1, N_HEADS, SAMPLE_ROWS, HEAD_DIM), lambda b, s, pt: (b, 0, 0, 0))
    k_page = (2 * N_HEADS, HEAD_DIM, page)

    def k_spec(p):
        return pl.BlockSpec((1, 1) + k_page, lambda b, s, pt: (layer, pt[b, s * npg + p], 0, 0, 0))

    def v_spec(p):
        return pl.BlockSpec((1, 1, v_rows, PAIR_DIM), lambda b, s, pt: (layer, pt[b, s * npg + p], 0, 0))

    grid_spec = pltpu.PrefetchScalarGridSpec(
        num_scalar_prefetch=1,
        grid=(db, n_pages // npg),
        in_specs=[vec, vec, vec, vec, pl.BlockSpec((1, PAIR_DIM), lambda b, s, pt: (0, 0)), qspec, qspec,
                  pl.BlockSpec((1,) + k_page, lambda b, s, pt: (b, 0, 0, 0)),
                  pl.BlockSpec((1, v_rows, PAIR_DIM), lambda b, s, pt: (b, 0, 0))]
                 + [k_spec(p) for p in range(npg)] + [v_spec(p) for p in range(npg)],
        out_specs=pl.BlockSpec((1, N_HEADS, SAMPLE_ROWS, PAIR_DIM), lambda b, s, pt: (b, 0, 0, 0)),
        scratch_shapes=[pltpu.VMEM((rows, 1), F32), pltpu.VMEM((rows, 1), F32),
                        pltpu.VMEM((rows, PAIR_DIM), F32)],
    )
    out = pl.pallas_call(
        functools.partial(_attn_sample_kernel, n_q=n_q, page=page, lam_init=lam_init),
        grid_spec=grid_spec,
        out_shape=jax.ShapeDtypeStruct((db, N_HEADS, SAMPLE_ROWS, PAIR_DIM), F32),
        compiler_params=_params("parallel", "arbitrary"),
        name="attn_sample",
    )(page_table, *lams, sub_gain, qa, qb, kn, vn, *([ck] * npg), *([cv] * npg))
    return out[:, :, :n_q].transpose(0, 2, 1, 3).reshape(db * n_q, D_ATTN).astype(BF16)


def _ln_silu(cf, g, b):
    mu = jnp.mean(cf, axis=-1, keepdims=True)
    d = cf - mu
    var = jnp.mean(d * d, axis=-1, keepdims=True)
    cn = d * lax.rsqrt(var + EPS) * g + b
    return cn * jax.nn.sigmoid(cn)


def _conv_prompt_kernel(h_ref, w_ref, b_ref, g_ref, bb_ref, o_ref, ext_ref, sh_ref, *, seq):
    lanes = 128
    n_sh = sh_ref.shape[1]
    ext_ref[pl.ds(0, CONV_HALO), :] = jnp.zeros((CONV_HALO, D_CONV), F32)
    ext_ref[pl.ds(CONV_HALO, seq), :] = h_ref[0]
    ext_ref[pl.ds(CONV_HALO + seq, 8), :] = jnp.zeros((8, D_CONV), F32)
    shift0 = CONV_HALO - (CONV_W - 1)
    n_tiles = seq // CONV_ROWS

    for c in range(D_CONV // lanes):
        cols = pl.ds(c * lanes, lanes)
        for r in range(8):
            sh_ref[r] = ext_ref[pl.ds(r, n_sh), cols]

        def tile(it, carry, cols=cols):
            t0 = pl.multiple_of(it * CONV_ROWS, 8)
            acc = jnp.zeros((CONV_ROWS, lanes), F32)
            for tap in range(CONV_W):
                a, r = divmod(tap + shift0, 8)
                acc = acc + sh_ref[r, pl.ds(t0 + 8 * a, CONV_ROWS), :] * w_ref[pl.ds(tap, 1), cols]
            ext_ref[pl.ds(CONV_HALO + t0, CONV_ROWS), cols] = acc
            return carry

        lax.fori_loop(0, n_tiles, tile, 0, unroll=2)

    def ln_tile(it, carry):
        t0 = pl.multiple_of(it * CONV_ROWS, 8)
        cf = ext_ref[pl.ds(CONV_HALO + t0, CONV_ROWS), :] + b_ref[...]
        o_ref[0, pl.ds(t0, CONV_ROWS), :] = _ln_silu(cf, g_ref[...], bb_ref[...]).astype(BF16)
        return carry

    lax.fori_loop(0, n_tiles, ln_tile, 0, unroll=4)


def _conv_prompt(h, dw_w, dw_b, ln_g, ln_b, batch, seq):
    h3 = h.reshape(batch, seq, D_CONV)
    row = pl.BlockSpec((1, D_CONV), lambda b: (0, 0))
    out = pl.pallas_call(
        functools.partial(_conv_prompt_kernel, seq=seq),
        grid=(batch,),
        in_specs=[pl.BlockSpec((1, seq, D_CONV), lambda b: (b, 0, 0)),
                  pl.BlockSpec((CONV_W, D_CONV), lambda b: (0, 0)), row, row, row],
        out_specs=pl.BlockSpec((1, seq, D_CONV), lambda b: (b, 0, 0)),
        out_shape=jax.ShapeDtypeStruct((batch, seq, D_CONV), BF16),
        scratch_shapes=[pltpu.VMEM((CONV_HALO + seq + 8, D_CONV), F32),
                        pltpu.VMEM((8, CONV_HALO + seq, 128), F32)],
        compiler_params=_params("parallel"),
        name="conv_prompt",
    )(h3, dw_w, dw_b, ln_g, ln_b)
    return out.reshape(batch * seq, D_CONV)


def _conv_sample_kernel(ext_ref, w_ref, b_ref, g_ref, bb_ref, o_ref, *, n_q):
    w = w_ref[...]
    for t in range(n_q):
        cf = jnp.sum(ext_ref[0, pl.ds(t, CONV_W), :] * w, axis=0, keepdims=True) + b_ref[...]
        o_ref[0, pl.ds(t, 1), :] = _ln_silu(cf, g_ref[...], bb_ref[...])


def _conv_sample(ext, dw_w, dw_b, ln_g, ln_b, n_q):
    db, rows, _ = ext.shape
    row = pl.BlockSpec((1, D_CONV), lambda b: (0, 0))
    out = pl.pallas_call(
        functools.partial(_conv_sample_kernel, n_q=n_q),
        grid=(db,),
        in_specs=[pl.BlockSpec((1, rows, D_CONV), lambda b: (b, 0, 0)),
                  pl.BlockSpec((CONV_W, D_CONV), lambda b: (0, 0)), row, row, row],
        out_specs=pl.BlockSpec((1, n_q, D_CONV), lambda b: (b, 0, 0)),
        out_shape=jax.ShapeDtypeStruct((db, n_q, D_CONV), F32),
        compiler_params=_params("parallel"),
        name="conv_sample",
    )(ext, dw_w, dw_b, ln_g, ln_b)
    return out.reshape(db * n_q, D_CONV).astype(BF16)


def _out_proj_kernel(x_ref, oa_ref, c_ref, wa_ref, wc_ref, o_ref):
    o_ref[...] = (x_ref[...]
                  + jnp.dot(oa_ref[...], wa_ref[...], preferred_element_type=F32)
                  + jnp.dot(c_ref[...], wc_ref[...], preferred_element_type=F32))


def _out_proj(x, oa, c, w_bf16, layer, tm):
    m = x.shape[0]
    tn = D_MODEL // 2
    return pl.pallas_call(
        _out_proj_kernel,
        grid=(m // tm, D_MODEL // tn),
        in_specs=[pl.BlockSpec((tm, tn), lambda i, j: (i, j)),
                  pl.BlockSpec((tm, D_ATTN), lambda i, j: (i, 0)),
                  pl.BlockSpec((tm, D_CONV), lambda i, j: (i, 0)),
                  pl.BlockSpec((None, D_ATTN, tn), lambda i, j: (layer, 0, j)),
                  pl.BlockSpec((None, D_CONV, tn), lambda i, j: (layer, D_ATTN // D_CONV, j))],
        out_specs=pl.BlockSpec((tm, tn), lambda i, j: (i, j)),
        out_shape=jax.ShapeDtypeStruct((m, D_MODEL), F32),
        compiler_params=_params("parallel", "arbitrary"),
        name="out_proj",
    )(x, oa, c, w_bf16, w_bf16)


def _ffn_kernel(x_ref, g_ref, wu_ref, wd_ref, o_ref, xn_ref):
    @pl.when(pl.program_id(1) == 0)
    def _():
        x = x_ref[...]
        xn_ref[...] = _rms_rows(x, g_ref[...]).astype(BF16)
        o_ref[...] = x

    h = jnp.dot(xn_ref[...], wu_ref[...], preferred_element_type=F32)
    a = jnp.square(jnp.maximum(h, 0.0)).astype(BF16)
    o_ref[...] += jnp.dot(a, wd_ref[...], preferred_element_type=F32)


def _ffn(x, gain, wu_bf16, wd_bf16, layer, tm, tf):
    m = x.shape[0]
    return pl.pallas_call(
        _ffn_kernel,
        grid=(m // tm, D_FF // tf),
        in_specs=[pl.BlockSpec((tm, D_MODEL), lambda i, j: (i, 0)),
                  pl.BlockSpec((1, D_MODEL), lambda i, j: (0, 0)),
                  pl.BlockSpec((None, D_MODEL, tf), lambda i, j: (layer, 0, j)),
                  pl.BlockSpec((None, tf, D_MODEL), lambda i, j: (layer, j, 0))],
        out_specs=pl.BlockSpec((tm, D_MODEL), lambda i, j: (i, 0)),
        out_shape=jax.ShapeDtypeStruct((m, D_MODEL), F32),
        scratch_shapes=[pltpu.VMEM((tm, D_MODEL), BF16)],
        compiler_params=_params("parallel", "arbitrary"),
        name="ffn",
    )(x, gain, wu_bf16, wd_bf16)


def _row_tile(m):
    for tm in (1032, 1024, 688, 512, 344, 256, 128, 64, 32, 16, 8):
        if m % tm == 0:
            return tm
    raise ValueError(f"unsupported row count {m}")


def kernel(x_prompt, x_sample, cache_k, cache_v, state_conv, page_table, meta_tokens, attn_norm, w_in,
           q_norm, k_norm, lambda_q1, lambda_k1, lambda_q2, lambda_k2, attn_subln, dw_w, dw_b,
           conv_ln_g, conv_ln_b, w_out, mlp_norm, w_up, w_down):
    depth = w_in.shape[0]
    b, seq_real, _ = x_prompt.shape
    seq = N_META + seq_real
    db, ds, _ = x_sample.shape

    w_in_b, w_out_b = w_in.astype(BF16), w_out.astype(BF16)
    w_up_b, w_down_b = w_up.astype(BF16), w_down.astype(BF16)
    heads_per_chunk = IN_CHUNK // HEAD_DIM
    blk = jnp.arange(IN_CHUNK) // HEAD_DIM
    gm = jnp.where(blk[:, None] == blk[None, :], 1.0 / HEAD_DIM, 0.0).astype(BF16)

    meta = jnp.broadcast_to(meta_tokens.astype(x_prompt.dtype)[None], (b, N_META, D_MODEL))
    xp = jnp.concatenate([meta, x_prompt], axis=1).reshape(b * seq, D_MODEL)
    xs = x_sample.reshape(db * ds, D_MODEL)
    tm_p, tm_s = _row_tile(b * seq), _row_tile(db * ds)

    kp, vp, cp, ks_, vs_, cs_ = [], [], [], [], [], []
    for l in range(depth):
        lam0 = lambda_init(l)
        lams = [a[l][None, :] for a in (lambda_q1, lambda_k1, lambda_q2, lambda_k2)]
        sub_gain = attn_subln[l][None, :]
        an, mn = attn_norm[l][None, :], mlp_norm[l][None, :]
        qg = jnp.tile(q_norm[l], heads_per_chunk)[None, :]
        kg = jnp.tile(k_norm[l], heads_per_chunk)[None, :]
        cw, cb = dw_w[l], dw_b[l][None, :]
        lg, lb = conv_ln_g[l][None, :], conv_ln_b[l][None, :]

        q, k, v, h = _in_proj(xp, an, w_in_b, l, gm, qg, kg, tm_p)
        oa, kt = _attn_prompt(lams, sub_gain, q, k, v, b, seq, lam0)
        c = _conv_prompt(h, cw, cb, lg, lb, b, seq)
        x1 = _out_proj(xp, oa, c, w_out_b, l, tm_p)
        xp = _ffn(x1, mn, w_up_b, w_down_b, l, tm_p, FFN_CHUNK)
        kp.append(kt)
        vp.append(v.reshape(b, seq, N_HEADS, PAIR_DIM))
        cp.append(h.reshape(b, seq, D_CONV)[:, seq - (CONV_W - 1):])

        q, k, v, h = _in_proj(xs, an, w_in_b, l, gm, qg, kg, tm_s)
        oa = _attn_sample(page_table, lams, sub_gain, q, k, v, cache_k, cache_v, l, lam0)
        ext = jnp.concatenate([state_conv[l], h.reshape(db, ds, D_CONV)], axis=1)
        c = _conv_sample(ext, cw, cb, lg, lb, ds)
        x1 = _out_proj(xs, oa, c, w_out_b, l, tm_s)
        xs = _ffn(x1, mn, w_up_b, w_down_b, l, tm_s, FFN_CHUNK)
        ks_.append(k.reshape(db, ds, 2 * N_HEADS, HEAD_DIM))
        vs_.append(v.reshape(db, ds, N_HEADS, PAIR_DIM))
        cs_.append(ext[:, ds:])

    y_prompt = xp.reshape(b, seq, D_MODEL)[:, N_META:]
    y_sample = xs.reshape(db, ds, D_MODEL)
    return (y_prompt, y_sample, jnp.stack(kp).transpose(0, 1, 4, 2, 3), jnp.stack(vp), jnp.stack(cp),
            jnp.stack(ks_), jnp.stack(vs_), jnp.stack(cs_))
```

```python
import functools
import math

import jax
import jax.numpy as jnp
from jax import lax
from jax.experimental import pallas as pl
from jax.experimental.pallas import tpu as pltpu

F32 = jnp.float32
BF16 = jnp.bfloat16

D_MODEL = 2048
N_META = 16
D_ATTN = D_MODEL // 2
N_HEADS = 8
HEAD_DIM = D_ATTN // (2 * N_HEADS)
PAIR_DIM = 2 * HEAD_DIM
D_CONV = D_MODEL - D_ATTN
CONV_W = 31
D_FF = 4 * D_MODEL
D_IN = 3 * D_ATTN + 2 * D_CONV
EPS = 1e-6
NEG_INF = -1e30
QK_SCALE = HEAD_DIM ** -0.5 * math.log2(math.e)

VMEM_LIMIT_BYTES = 56 * 1024 * 1024

IN_CHUNK = 512
FFN_CHUNK = 512
ATTN_TILE = 256
ATTN_CHUNK = 256
ATTN_TAIL_TILE = 128
SUM_ROWS = 16
CONV_ROWS = 48
CONV_HALO = 32
PAGES_PER_STEP = 8

def lambda_init(l):
    return 0.8 - 0.6 * math.exp(-0.3 * l)


def _params(*semantics):
    return pltpu.CompilerParams(dimension_semantics=semantics, vmem_limit_bytes=VMEM_LIMIT_BYTES)


def _rms_rows(x, gain):
    ms = jnp.mean(x * x, axis=-1, keepdims=True)
    return x * lax.rsqrt(ms + EPS) * gain


def _diff_lambda(lq1, lk1, lq2, lk2, lam_init):
    e1 = jnp.exp(jnp.sum(lq1[...] * lk1[...], axis=-1, keepdims=True))
    e2 = jnp.exp(jnp.sum(lq2[...] * lk2[...], axis=-1, keepdims=True))
    return e1 - e2 + lam_init


def _in_proj_kernel(x_ref, g_ref, w_ref, wg_ref, gm_ref, qg_ref, kg_ref,
                    q_ref, k_ref, v_ref, h_ref, xn_ref):
    j = pl.program_id(1)
    n_q = D_ATTN // IN_CHUNK

    @pl.when(j == 0)
    def _():
        xn_ref[...] = _rms_rows(x_ref[...], g_ref[...]).astype(BF16)

    xn = xn_ref[...]
    z = jnp.dot(xn, w_ref[...], preferred_element_type=F32)

    def head_norm(gain):
        ms = jnp.dot((z * z).astype(BF16), gm_ref[...], preferred_element_type=F32)
        return z * lax.rsqrt(ms + EPS) * gain

    @pl.when(j < n_q)
    def _():
        q_ref[...] = (head_norm(qg_ref[...]) * QK_SCALE).astype(BF16)

    @pl.when((j >= n_q) & (j < 2 * n_q))
    def _():
        k_ref[...] = head_norm(kg_ref[...])

    @pl.when((j >= 2 * n_q) & (j < 3 * n_q))
    def _():
        v_ref[...] = z

    @pl.when(j >= 3 * n_q)
    def _():
        gate = jnp.dot(xn, wg_ref[...], preferred_element_type=F32)
        h_ref[...] = z * jax.nn.sigmoid(gate)


def _in_proj(x, gain, w_bf16, layer, gm, qg, kg, tm):
    m = x.shape[0]
    n_q = D_ATTN // IN_CHUNK
    n_steps = 4 * n_q
    gate0 = (3 * D_ATTN + D_CONV) // IN_CHUNK

    def region(r):
        return lambda i, j: (i, jnp.clip(j - r * n_q, 0, n_q - 1))

    return pl.pallas_call(
        _in_proj_kernel,
        grid=(m // tm, n_steps),
        in_specs=[
            pl.BlockSpec((tm, D_MODEL), lambda i, j: (i, 0)),
            pl.BlockSpec((1, D_MODEL), lambda i, j: (0, 0)),
            pl.BlockSpec((None, D_MODEL, IN_CHUNK), lambda i, j: (layer, 0, j)),
            pl.BlockSpec((None, D_MODEL, IN_CHUNK), lambda i, j: (layer, 0, gate0 + jnp.maximum(j - 3 * n_q, 0))),
            pl.BlockSpec((IN_CHUNK, IN_CHUNK), lambda i, j: (0, 0)),
            pl.BlockSpec((1, IN_CHUNK), lambda i, j: (0, 0)),
            pl.BlockSpec((1, IN_CHUNK), lambda i, j: (0, 0)),
        ],
        out_specs=[
            pl.BlockSpec((tm, IN_CHUNK), region(0)),
            pl.BlockSpec((tm, IN_CHUNK), region(1)),
            pl.BlockSpec((tm, IN_CHUNK), region(2)),
            pl.BlockSpec((tm, IN_CHUNK), region(3)),
        ],
        out_shape=[
            jax.ShapeDtypeStruct((m, D_ATTN), BF16),
            jax.ShapeDtypeStruct((m, D_ATTN), F32),
            jax.ShapeDtypeStruct((m, D_ATTN), F32),
            jax.ShapeDtypeStruct((m, D_CONV), F32),
        ],
        scratch_shapes=[pltpu.VMEM((tm, D_MODEL), BF16)],
        compiler_params=_params("parallel", "arbitrary"),
        name="in_proj",
    )(x, gain, w_bf16, w_bf16, gm, qg, kg)


def _sub_ln(o, gain, lam_init):
    ms = jnp.mean(o * o, axis=-1, keepdims=True)
    return o * lax.rsqrt(ms + EPS) * gain * (1.0 - lam_init)


def _scores_t(kc, qq):
    return lax.dot_general(kc, qq, (((1,), (1,)), ((), ())), preferred_element_type=F32)


def _softmax_t(s, m):
    m_new = jnp.maximum(m, jnp.max(s, axis=0, keepdims=True))
    return m_new, jnp.exp2(m - m_new), jnp.exp2(s - m_new).astype(BF16)


def _attn_prompt_kernel(lq1, lk1, lq2, lk2, sgt_ref, q_ref, k_ref, v_ref, o_ref, kt_ref,
                        kb_ref, vt_ref, *, seq, lam_init):
    lam = _diff_lambda(lq1, lk1, lq2, lk2, lam_init)
    ct, tq = ATTN_CHUNK, ATTN_TILE
    n_chunks = kb_ref.shape[0] // ct
    pad = n_chunks * ct - seq
    kb_ref[pl.ds(0, seq), :] = k_ref[0].astype(BF16)
    kb_ref[pl.ds(seq, pad), :] = jnp.zeros((pad, PAIR_DIM), BF16)
    for c in range(n_chunks):
        rows = min(ct, seq - c * ct)
        vc, kc = v_ref[0, pl.ds(c * ct, rows), :], k_ref[0, pl.ds(c * ct, rows), :]
        if rows < ct:
            fill = jnp.zeros((ct - rows, PAIR_DIM), F32)
            vc, kc = jnp.concatenate([vc, fill], axis=0), jnp.concatenate([kc, fill], axis=0)
        ones_row = (lax.broadcasted_iota(jnp.int32, (SUM_ROWS, ct), 0) == 0).astype(F32)
        vt_ref[c] = jnp.concatenate([vc.T, ones_row], axis=0).astype(BF16)
        kt_ref[0, :, pl.ds(c * ct, rows)] = kc.T[:, :rows]

    first = lax.broadcasted_iota(jnp.int32, (1, PAIR_DIM), 1) < HEAD_DIM
    n_full, tail = divmod(seq, tq)
    for i in range(pl.cdiv(seq, tq)):
        r0 = i * tq
        n_before = r0 // ct
        if i < n_full:
            t, valid = tq, tq
            qt = q_ref[0, pl.ds(r0, t), :]
        else:
            t, valid = ATTN_TAIL_TILE, tail
            qt = jnp.concatenate([q_ref[0, pl.ds(r0, tail), :], jnp.zeros((t - tail, PAIR_DIM), BF16)], axis=0)
        zero = jnp.zeros_like(qt)
        qq = jnp.concatenate([jnp.where(first, qt, zero), jnp.where(first, zero, qt)], axis=0)
        def scores(c, qq=qq):
            return _scores_t(kb_ref[pl.ds(c * ct, ct), :], qq)

        s = scores(0)
        m = jnp.full((1, 2 * t), NEG_INF, F32)
        acc, p_prev = jnp.zeros((PAIR_DIM + SUM_ROWS, 2 * t), F32), None
        for c in range(n_before + 1):
            s_next = None
            if c == n_before:
                kpos = c * ct + lax.broadcasted_iota(jnp.int32, (ct, 2 * t), 0)
                qpos = r0 + lax.broadcasted_iota(jnp.int32, (ct, 2 * t), 1) % t
                s = jnp.where(kpos <= qpos, s, NEG_INF)
            else:
                s_next = scores(c + 1)
            m, alpha, p = _softmax_t(s, m)
            if p_prev is not None:
                acc = alpha * (acc + jnp.dot(vt_ref[c - 1], p_prev, preferred_element_type=F32))
            s, p_prev = s_next, p
        acc = acc + jnp.dot(vt_ref[n_before], p_prev, preferred_element_type=F32)
        l, acc = acc[PAIR_DIM:PAIR_DIM + 1], acc[:PAIR_DIM]
        ot = acc[:, :t] / l[:, :t] - lam * (acc[:, t:] / l[:, t:])
        ms = jnp.mean(ot * ot, axis=0, keepdims=True)
        ot = ot * lax.rsqrt(ms + EPS) * sgt_ref[...] * (1.0 - lam_init)
        o_ref[0, pl.ds(r0, valid), :] = ot.T[:valid].astype(BF16)


def _attn_prompt(lams, sub_gain, q, k, v, batch, seq, lam_init):
    q3 = q.reshape(batch, seq, D_ATTN)
    k3 = k.reshape(batch, seq, D_ATTN)
    v3 = v.reshape(batch, seq, D_ATTN)
    n_chunks = pl.cdiv(seq, ATTN_CHUNK)
    assert seq % ATTN_TILE <= ATTN_TAIL_TILE and ATTN_CHUNK % ATTN_TILE == 0
    vec = pl.BlockSpec((1, HEAD_DIM), lambda b, h: (0, 0))
    blk = pl.BlockSpec((1, seq, PAIR_DIM), lambda b, h: (b, 0, h))
    out, kt = pl.pallas_call(
        functools.partial(_attn_prompt_kernel, seq=seq, lam_init=lam_init),
        grid=(batch, N_HEADS),
        in_specs=[vec, vec, vec, vec, pl.BlockSpec((PAIR_DIM, 1), lambda b, h: (0, 0)), blk, blk, blk],
        out_specs=[blk, pl.BlockSpec((1, PAIR_DIM, seq), lambda b, h: (b, h, 0))],
        out_shape=[jax.ShapeDtypeStruct((batch, seq, D_ATTN), BF16),
                   jax.ShapeDtypeStruct((batch, D_ATTN, seq), F32)],
        scratch_shapes=[pltpu.VMEM((n_chunks * ATTN_CHUNK, PAIR_DIM), BF16),
                        pltpu.VMEM((n_chunks, PAIR_DIM + SUM_ROWS, ATTN_CHUNK), BF16)],
        compiler_params=_params("parallel", "parallel"),
        name="attn_prompt",
    )(*lams, sub_gain.reshape(PAIR_DIM, 1), q3, k3, v3)
    return out.reshape(batch * seq, D_ATTN), kt.reshape(batch, 2 * N_HEADS, HEAD_DIM, seq)


SAMPLE_ROWS = 8


def _attn_sample_kernel(pt_ref, lq1, lk1, lq2, lk2, sg_ref, qa_ref, qb_ref, kn_ref, vn_ref, *rest,
                        n_q, page, lam_init):
    del pt_ref
    npg = PAGES_PER_STEP
    k_refs, v_refs = rest[:npg], rest[npg:2 * npg]
    o_ref, m_ref, l_ref, acc_ref = rest[2 * npg:]
    s_idx = pl.program_id(1)
    rows = N_HEADS * SAMPLE_ROWS
    half = SAMPLE_ROWS // 2

    @pl.when(s_idx == 0)
    def _():
        m_ref[...] = jnp.full(m_ref.shape, NEG_INF, F32)
        l_ref[...] = jnp.zeros(l_ref.shape, F32)
        acc_ref[...] = jnp.zeros(acc_ref.shape, F32)

    qa = [qa_ref[0, h].astype(BF16) for h in range(N_HEADS)]
    qb = [qb_ref[0, h].astype(BF16) for h in range(N_HEADS)]

    def scores(k_head):
        blocks = []
        for h in range(N_HEADS):
            s1 = jnp.dot(qa[h], k_head(2 * h).astype(BF16), preferred_element_type=F32)
            s2 = jnp.dot(qb[h], k_head(2 * h + 1).astype(BF16), preferred_element_type=F32)
            blocks.append(s1 + s2)
        return jnp.concatenate(blocks, axis=0)

    def update(s, v_heads):
        m = m_ref[...]
        m_new = jnp.maximum(m, jnp.max(s, axis=-1, keepdims=True))
        alpha = jnp.exp2(m - m_new)
        p = jnp.exp2(s - m_new)
        l_ref[...] = alpha * l_ref[...] + jnp.sum(p, axis=-1, keepdims=True)
        pv = []
        for h in range(N_HEADS):
            ph = p[h * SAMPLE_ROWS:(h + 1) * SAMPLE_ROWS]
            acc = None
            for i, v_head in enumerate(v_heads):
                d = jnp.dot(ph[:, i * page:(i + 1) * page].astype(BF16), v_head(h).astype(BF16),
                            preferred_element_type=F32)
                acc = d if acc is None else acc + d
            pv.append(acc)
        acc_ref[...] = alpha * acc_ref[...] + jnp.concatenate(pv, axis=0)
        m_ref[...] = m_new

    def k_slice(ref, lead):
        return lambda h16: ref[(*lead, h16)]

    def v_slice(ref, lead):

        return lambda h: ref[(*lead, pl.ds(h, page, stride=N_HEADS), slice(None))]

    s = jnp.concatenate([scores(k_slice(kr, (0, 0))) for kr in k_refs], axis=1)
    update(s, [v_slice(vr, (0, 0)) for vr in v_refs])

    @pl.when(s_idx == pl.num_programs(1) - 1)
    def _():
        qi = lax.broadcasted_iota(jnp.int32, (rows, page), 0) % half
        kj = lax.broadcasted_iota(jnp.int32, (rows, page), 1)
        s_new = scores(k_slice(kn_ref, (0,)))
        update(jnp.where(kj <= qi, s_new, NEG_INF), [v_slice(vn_ref, (0,))])

        f = acc_ref[...] / l_ref[...]
        lam = _diff_lambda(lq1, lk1, lq2, lk2, lam_init)
        for h in range(N_HEADS):
            r0 = h * SAMPLE_ROWS
            o = f[r0:r0 + SAMPLE_ROWS] - lam * pltpu.roll(f[r0:r0 + SAMPLE_ROWS], half, 0)
            o_ref[0, h] = _sub_ln(o, sg_ref[...], lam_init)


def _attn_sample(page_table, lams, sub_gain, q, k_new, v_new, cache_k, cache_v, layer, lam_init):
    db, n_pages = page_table.shape
    n_q = q.shape[0] // db
    page = cache_k.shape[2]
    npg = PAGES_PER_STEP
    rows = N_HEADS * SAMPLE_ROWS
    half = SAMPLE_ROWS // 2
    assert n_q <= half

    q5 = q.astype(F32).reshape(db, n_q, N_HEADS, 2, HEAD_DIM).transpose(0, 2, 3, 1, 4)
    q5 = jnp.pad(q5, ((0, 0), (0, 0), (0, 0), (0, half - n_q), (0, 0)))
    zeros = jnp.zeros_like(q5[:, :, 0])
    qa = jnp.concatenate([q5[:, :, 0], zeros], axis=2)
    qb = jnp.concatenate([zeros, q5[:, :, 1]], axis=2)
    v_rows = page * N_HEADS
    kn = jnp.pad(k_new.reshape(db, n_q, 2 * N_HEADS, HEAD_DIM).transpose(0, 2, 3, 1),
                 ((0, 0), (0, 0), (0, 0), (0, page - n_q)))
    vn = jnp.pad(v_new.reshape(db, n_q * N_HEADS, PAIR_DIM), ((0, 0), (0, v_rows - n_q * N_HEADS), (0, 0)))
    ck = cache_k.transpose(0, 1, 3, 4, 2)
    cv = cache_v.reshape(cache_v.shape[0], cache_v.shape[1], v_rows, PAIR_DIM)

    vec = pl.BlockSpec((1, HEAD_DIM), lambda b, s, pt: (0, 0))
    qspec = pl.BlockSpec((1, N_HEADS, SAMPLE_ROWS, HEAD_DIM), lambda b, s, pt: (b, 0, 0, 0))
    k_page = (2 * N_HEADS, HEAD_DIM, page)

    def k_spec(p):
        return pl.BlockSpec((1, 1) + k_page, lambda b, s, pt: (layer, pt[b, s * npg + p], 0, 0, 0))

    def v_spec(p):
        return pl.BlockSpec((1, 1, v_rows, PAIR_DIM), lambda b, s, pt: (layer, pt[b, s * npg + p], 0, 0))

    grid_spec = pltpu.PrefetchScalarGridSpec(
        num_scalar_prefetch=1,
        grid=(db, n_pages // npg),
        in_specs=[vec, vec, vec, vec, pl.BlockSpec((1, PAIR_DIM), lambda b, s, pt: (0, 0)), qspec, qspec,
                  pl.BlockSpec((1,) + k_page, lambda b, s, pt: (b, 0, 0, 0)),
                  pl.BlockSpec((1, v_rows, PAIR_DIM), lambda b, s, pt: (b, 0, 0))]
                 + [k_spec(p) for p in range(npg)] + [v_spec(p) for p in range(npg)],
        out_specs=pl.BlockSpec((1, N_HEADS, SAMPLE_ROWS, PAIR_DIM), lambda b, s, pt: (b, 0, 0, 0)),
        scratch_shapes=[pltpu.VMEM((rows, 1), F32), pltpu.VMEM((rows, 1), F32),
                        pltpu.VMEM((rows, PAIR_DIM), F32)],
    )
    out = pl.pallas_call(
        functools.partial(_attn_sample_kernel, n_q=n_q, page=page, lam_init=lam_init),
        grid_spec=grid_spec,
        out_shape=jax.ShapeDtypeStruct((db, N_HEADS, SAMPLE_ROWS, PAIR_DIM), F32),
        compiler_params=_params("parallel", "arbitrary"),
        name="attn_sample",
    )(page_table, *lams, sub_gain, qa, qb, kn, vn, *([ck] * npg), *([cv] * npg))
    return out[:, :, :n_q].transpose(0, 2, 1, 3).reshape(db * n_q, D_ATTN).astype(BF16)


def _ln_silu(cf, g, b):
    mu = jnp.mean(cf, axis=-1, keepdims=True)
    d = cf - mu
    var = jnp.mean(d * d, axis=-1, keepdims=True)
    cn = d * lax.rsqrt(var + EPS) * g + b
    return cn * jax.nn.sigmoid(cn)


def _conv_prompt_kernel(h_ref, w_ref, b_ref, g_ref, bb_ref, o_ref, ext_ref, sh_ref, *, seq):
    lanes = 128
    n_sh = sh_ref.shape[1]
    ext_ref[pl.ds(0, CONV_HALO), :] = jnp.zeros((CONV_HALO, D_CONV), F32)
    ext_ref[pl.ds(CONV_HALO, seq), :] = h_ref[0]
    ext_ref[pl.ds(CONV_HALO + seq, 8), :] = jnp.zeros((8, D_CONV), F32)
    shift0 = CONV_HALO - (CONV_W - 1)
    n_tiles = seq // CONV_ROWS

    for c in range(D_CONV // lanes):
        cols = pl.ds(c * lanes, lanes)
        for r in range(8):
            sh_ref[r] = ext_ref[pl.ds(r, n_sh), cols]

        def tile(it, carry, cols=cols):
            t0 = pl.multiple_of(it * CONV_ROWS, 8)
            acc = jnp.zeros((CONV_ROWS, lanes), F32)
            for tap in range(CONV_W):
                a, r = divmod(tap + shift0, 8)
                acc = acc + sh_ref[r, pl.ds(t0 + 8 * a, CONV_ROWS), :] * w_ref[pl.ds(tap, 1), cols]
            ext_ref[pl.ds(CONV_HALO + t0, CONV_ROWS), cols] = acc
            return carry

        lax.fori_loop(0, n_tiles, tile, 0, unroll=2)

    def ln_tile(it, carry):
        t0 = pl.multiple_of(it * CONV_ROWS, 8)
        cf = ext_ref[pl.ds(CONV_HALO + t0, CONV_ROWS), :] + b_ref[...]
        o_ref[0, pl.ds(t0, CONV_ROWS), :] = _ln_silu(cf, g_ref[...], bb_ref[...]).astype(BF16)
        return carry

    lax.fori_loop(0, n_tiles, ln_tile, 0, unroll=4)


def _conv_prompt(h, dw_w, dw_b, ln_g, ln_b, batch, seq):
    h3 = h.reshape(batch, seq, D_CONV)
    row = pl.BlockSpec((1, D_CONV), lambda b: (0, 0))
    out = pl.pallas_call(
        functools.partial(_conv_prompt_kernel, seq=seq),
        grid=(batch,),
        in_specs=[pl.BlockSpec((1, seq, D_CONV), lambda b: (b, 0, 0)),
                  pl.BlockSpec((CONV_W, D_CONV), lambda b: (0, 0)), row, row, row],
        out_specs=pl.BlockSpec((1, seq, D_CONV), lambda b: (b, 0, 0)),
        out_shape=jax.ShapeDtypeStruct((batch, seq, D_CONV), BF16),
        scratch_shapes=[pltpu.VMEM((CONV_HALO + seq + 8, D_CONV), F32),
                        pltpu.VMEM((8, CONV_HALO + seq, 128), F32)],
        compiler_params=_params("parallel"),
        name="conv_prompt",
    )(h3, dw_w, dw_b, ln_g, ln_b)
    return out.reshape(batch * seq, D_CONV)


def _conv_sample_kernel(ext_ref, w_ref, b_ref, g_ref, bb_ref, o_ref, *, n_q):
    w = w_ref[...]
    for t in range(n_q):
        cf = jnp.sum(ext_ref[0, pl.ds(t, CONV_W), :] * w, axis=0, keepdims=True) + b_ref[...]
        o_ref[0, pl.ds(t, 1), :] = _ln_silu(cf, g_ref[...], bb_ref[...])


def _conv_sample(ext, dw_w, dw_b, ln_g, ln_b, n_q):
    db, rows, _ = ext.shape
    row = pl.BlockSpec((1, D_CONV), lambda b: (0, 0))
    out = pl.pallas_call(
        functools.partial(_conv_sample_kernel, n_q=n_q),
        grid=(db,),
        in_specs=[pl.BlockSpec((1, rows, D_CONV), lambda b: (b, 0, 0)),
                  pl.BlockSpec((CONV_W, D_CONV), lambda b: (0, 0)), row, row, row],
        out_specs=pl.BlockSpec((1, n_q, D_CONV), lambda b: (b, 0, 0)),
        out_shape=jax.ShapeDtypeStruct((db, n_q, D_CONV), F32),
        compiler_params=_params("parallel"),
        name="conv_sample",
    )(ext, dw_w, dw_b, ln_g, ln_b)
    return out.reshape(db * n_q, D_CONV).astype(BF16)


def _out_proj_kernel(x_ref, oa_ref, c_ref, wa_ref, wc_ref, o_ref):
    o_ref[...] = (x_ref[...]
                  + jnp.dot(oa_ref[...], wa_ref[...], preferred_element_type=F32)
                  + jnp.dot(c_ref[...], wc_ref[...], preferred_element_type=F32))


def _out_proj(x, oa, c, w_bf16, layer, tm):
    m = x.shape[0]
    tn = D_MODEL // 2
    return pl.pallas_call(
        _out_proj_kernel,
        grid=(m // tm, D_MODEL // tn),
        in_specs=[pl.BlockSpec((tm, tn), lambda i, j: (i, j)),
                  pl.BlockSpec((tm, D_ATTN), lambda i, j: (i, 0)),
                  pl.BlockSpec((tm, D_CONV), lambda i, j: (i, 0)),
                  pl.BlockSpec((None, D_ATTN, tn), lambda i, j: (layer, 0, j)),
                  pl.BlockSpec((None, D_CONV, tn), lambda i, j: (layer, D_ATTN // D_CONV, j))],
        out_specs=pl.BlockSpec((tm, tn), lambda i, j: (i, j)),
        out_shape=jax.ShapeDtypeStruct((m, D_MODEL), F32),
        compiler_params=_params("parallel", "arbitrary"),
        name="out_proj",
    )(x, oa, c, w_bf16, w_bf16)


def _ffn_kernel(x_ref, g_ref, wu_ref, wd_ref, o_ref, xn_ref):
    @pl.when(pl.program_id(1) == 0)
    def _():
        x = x_ref[...]
        xn_ref[...] = _rms_rows(x, g_ref[...]).astype(BF16)
        o_ref[...] = x

    h = jnp.dot(xn_ref[...], wu_ref[...], preferred_element_type=F32)
    a = jnp.square(jnp.maximum(h, 0.0)).astype(BF16)
    o_ref[...] += jnp.dot(a, wd_ref[...], preferred_element_type=F32)


def _ffn(x, gain, wu_bf16, wd_bf16, layer, tm, tf):
    m = x.shape[0]
    return pl.pallas_call(
        _ffn_kernel,
        grid=(m // tm, D_FF // tf),
        in_specs=[pl.BlockSpec((tm, D_MODEL), lambda i, j: (i, 0)),
                  pl.BlockSpec((1, D_MODEL), lambda i, j: (0, 0)),
                  pl.BlockSpec((None, D_MODEL, tf), lambda i, j: (layer, 0, j)),
                  pl.BlockSpec((None, tf, D_MODEL), lambda i, j: (layer, j, 0))],
        out_specs=pl.BlockSpec((tm, D_MODEL), lambda i, j: (i, 0)),
        out_shape=jax.ShapeDtypeStruct((m, D_MODEL), F32),
        scratch_shapes=[pltpu.VMEM((tm, D_MODEL), BF16)],
        compiler_params=_params("parallel", "arbitrary"),
        name="ffn",
    )(x, gain, wu_bf16, wd_bf16)


def _row_tile(m):
    for tm in (1032, 1024, 688, 512, 344, 256, 128, 64, 32, 16, 8):
        if m % tm == 0:
            return tm
    raise ValueError(f"unsupported row count {m}")


def kernel(x_prompt, x_sample, cache_k, cache_v, state_conv, page_table, meta_tokens, attn_norm, w_in,
           q_norm, k_norm, lambda_q1, lambda_k1, lambda_q2, lambda_k2, attn_subln, dw_w, dw_b,
           conv_ln_g, conv_ln_b, w_out, mlp_norm, w_up, w_down):
    depth = w_in.shape[0]
    b, seq_real, _ = x_prompt.shape
    seq = N_META + seq_real
    db, ds, _ = x_sample.shape

    w_in_b, w_out_b = w_in.astype(BF16), w_out.astype(BF16)
    w_up_b, w_down_b = w_up.astype(BF16), w_down.astype(BF16)
    heads_per_chunk = IN_CHUNK // HEAD_DIM
    blk = jnp.arange(IN_CHUNK) // HEAD_DIM
    gm = jnp.where(blk[:, None] == blk[None, :], 1.0 / HEAD_DIM, 0.0).astype(BF16)

    meta = jnp.broadcast_to(meta_tokens.astype(x_prompt.dtype)[None], (b, N_META, D_MODEL))
    xp = jnp.concatenate([meta, x_prompt], axis=1).reshape(b * seq, D_MODEL)
    xs = x_sample.reshape(db * ds, D_MODEL)
    tm_p, tm_s = _row_tile(b * seq), _row_tile(db * ds)

    kp, vp, cp, ks_, vs_, cs_ = [], [], [], [], [], []
    for l in range(depth):
        lam0 = lambda_init(l)
        lams = [a[l][None, :] for a in (lambda_q1, lambda_k1, lambda_q2, lambda_k2)]
        sub_gain = attn_subln[l][None, :]
        an, mn = attn_norm[l][None, :], mlp_norm[l][None, :]
        qg = jnp.tile(q_norm[l], heads_per_chunk)[None, :]
        kg = jnp.tile(k_norm[l], heads_per_chunk)[None, :]
        cw, cb = dw_w[l], dw_b[l][None, :]
        lg, lb = conv_ln_g[l][None, :], conv_ln_b[l][None, :]

        q, k, v, h = _in_proj(xp, an, w_in_b, l, gm, qg, kg, tm_p)
        oa, kt = _attn_prompt(lams, sub_gain, q, k, v, b, seq, lam0)
        c = _conv_prompt(h, cw, cb, lg, lb, b, seq)
        x1 = _out_proj(xp, oa, c, w_out_b, l, tm_p)
        xp = _ffn(x1, mn, w_up_b, w_down_b, l, tm_p, FFN_CHUNK)
        kp.append(kt)
        vp.append(v.reshape(b, seq, N_HEADS, PAIR_DIM))
        cp.append(h.reshape(b, seq, D_CONV)[:, seq - (CONV_W - 1):])

        q, k, v, h = _in_proj(xs, an, w_in_b, l, gm, qg, kg, tm_s)
        oa = _attn_sample(page_table, lams, sub_gain, q, k, v, cache_k, cache_v, l, lam0)
        ext = jnp.concatenate([state_conv[l], h.reshape(db, ds, D_CONV)], axis=1)
        c = _conv_sample(ext, cw, cb, lg, lb, ds)
        x1 = _out_proj(xs, oa, c, w_out_b, l, tm_s)
        xs = _ffn(x1, mn, w_up_b, w_down_b, l, tm_s, FFN_CHUNK)
        ks_.append(k.reshape(db, ds, 2 * N_HEADS, HEAD_DIM))
        vs_.append(v.reshape(db, ds, N_HEADS, PAIR_DIM))
        cs_.append(ext[:, ds:])

    y_prompt = xp.reshape(b, seq, D_MODEL)[:, N_META:]
    y_sample = xs.reshape(db, ds, D_MODEL)
    return (y_prompt, y_sample, jnp.stack(kp).transpose(0, 1, 4, 2, 3), jnp.stack(vp), jnp.stack(cp),
            jnp.stack(ks_), jnp.stack(vs_), jnp.stack(cs_))
```

```python
import functools
import math

import jax
import jax.numpy as jnp
from jax import lax
from jax.experimental import pallas as pl
from jax.experimental.pallas import tpu as pltpu

F32 = jnp.float32
BF16 = jnp.bfloat16

D_MODEL = 2048
N_META = 16
D_ATTN = D_MODEL // 2
N_HEADS = 8
HEAD_DIM = D_ATTN // (2 * N_HEADS)
PAIR_DIM = 2 * HEAD_DIM
D_CONV = D_MODEL - D_ATTN
CONV_W = 31
D_FF = 4 * D_MODEL
D_IN = 3 * D_ATTN + 2 * D_CONV
EPS = 1e-6
NEG_INF = -1e30
QK_SCALE = HEAD_DIM ** -0.5 * math.log2(math.e)

VMEM_LIMIT_BYTES = 56 * 1024 * 1024

IN_CHUNK = 512
FFN_CHUNK = 512
ATTN_TILE = 256
ATTN_CHUNK = 256
ATTN_TAIL_TILE = 128
SUM_ROWS = 16
CONV_ROWS = 48
CONV_HALO = 32
PAGES_PER_STEP = 16

def lambda_init(l):
    return 0.8 - 0.6 * math.exp(-0.3 * l)


def _params(*semantics):
    return pltpu.CompilerParams(dimension_semantics=semantics, vmem_limit_bytes=VMEM_LIMIT_BYTES)


def _rms_rows(x, gain):
    ms = jnp.mean(x * x, axis=-1, keepdims=True)
    return x * lax.rsqrt(ms + EPS) * gain


def _diff_lambda(lq1, lk1, lq2, lk2, lam_init):
    e1 = jnp.exp(jnp.sum(lq1[...] * lk1[...], axis=-1, keepdims=True))
    e2 = jnp.exp(jnp.sum(lq2[...] * lk2[...], axis=-1, keepdims=True))
    return e1 - e2 + lam_init


def _in_proj_kernel(x_ref, g_ref, w_ref, wg_ref, gm_ref, qg_ref, kg_ref,
                    q_ref, k_ref, v_ref, h_ref, xn_ref):
    j = pl.program_id(1)
    n_q = D_ATTN // IN_CHUNK

    @pl.when(j == 0)
    def _():
        xn_ref[...] = _rms_rows(x_ref[...], g_ref[...]).astype(BF16)

    xn = xn_ref[...]
    z = jnp.dot(xn, w_ref[...], preferred_element_type=F32)

    def head_norm(gain):
        ms = jnp.dot((z * z).astype(BF16), gm_ref[...], preferred_element_type=F32)
        return z * lax.rsqrt(ms + EPS) * gain

    @pl.when(j < n_q)
    def _():
        q_ref[...] = (head_norm(qg_ref[...]) * QK_SCALE).astype(BF16)

    @pl.when((j >= n_q) & (j < 2 * n_q))
    def _():
        k_ref[...] = head_norm(kg_ref[...])

    @pl.when((j >= 2 * n_q) & (j < 3 * n_q))
    def _():
        v_ref[...] = z

    @pl.when(j >= 3 * n_q)
    def _():
        gate = jnp.dot(xn, wg_ref[...], preferred_element_type=F32)
        h_ref[...] = z * jax.nn.sigmoid(gate)


def _in_proj(x, gain, w_bf16, layer, gm, qg, kg, tm):
    m = x.shape[0]
    n_q = D_ATTN // IN_CHUNK
    n_steps = 4 * n_q
    gate0 = (3 * D_ATTN + D_CONV) // IN_CHUNK

    def region(r):
        return lambda i, j: (i, jnp.clip(j - r * n_q, 0, n_q - 1))

    return pl.pallas_call(
        _in_proj_kernel,
        grid=(m // tm, n_steps),
        in_specs=[
            pl.BlockSpec((tm, D_MODEL), lambda i, j: (i, 0)),
            pl.BlockSpec((1, D_MODEL), lambda i, j: (0, 0)),
            pl.BlockSpec((None, D_MODEL, IN_CHUNK), lambda i, j: (layer, 0, j)),
            pl.BlockSpec((None, D_MODEL, IN_CHUNK), lambda i, j: (layer, 0, gate0 + jnp.maximum(j - 3 * n_q, 0))),
            pl.BlockSpec((IN_CHUNK, IN_CHUNK), lambda i, j: (0, 0)),
            pl.BlockSpec((1, IN_CHUNK), lambda i, j: (0, 0)),
            pl.BlockSpec((1, IN_CHUNK), lambda i, j: (0, 0)),
        ],
        out_specs=[
            pl.BlockSpec((tm, IN_CHUNK), region(0)),
            pl.BlockSpec((tm, IN_CHUNK), region(1)),
            pl.BlockSpec((tm, IN_CHUNK), region(2)),
            pl.BlockSpec((tm, IN_CHUNK), region(3)),
        ],
        out_shape=[
            jax.ShapeDtypeStruct((m, D_ATTN), BF16),
            jax.ShapeDtypeStruct((m, D_ATTN), F32),
            jax.ShapeDtypeStruct((m, D_ATTN), F32),
            jax.ShapeDtypeStruct((m, D_CONV), F32),
        ],
        scratch_shapes=[pltpu.VMEM((tm, D_MODEL), BF16)],
        compiler_params=_params("parallel", "arbitrary"),
        name="in_proj",
    )(x, gain, w_bf16, w_bf16, gm, qg, kg)


def _sub_ln(o, gain, lam_init):
    ms = jnp.mean(o * o, axis=-1, keepdims=True)
    return o * lax.rsqrt(ms + EPS) * gain * (1.0 - lam_init)


def _scores_t(kc, qq):
    return lax.dot_general(kc, qq, (((1,), (1,)), ((), ())), preferred_element_type=F32)


def _softmax_t(s, m):
    m_new = jnp.maximum(m, jnp.max(s, axis=0, keepdims=True))
    return m_new, jnp.exp2(m - m_new), jnp.exp2(s - m_new).astype(BF16)


def _attn_prompt_kernel(lq1, lk1, lq2, lk2, sgt_ref, q_ref, k_ref, v_ref, o_ref, kt_ref,
                        kb_ref, vt_ref, *, seq, lam_init):
    lam = _diff_lambda(lq1, lk1, lq2, lk2, lam_init)
    ct, tq = ATTN_CHUNK, ATTN_TILE
    n_chunks = kb_ref.shape[0] // ct
    pad = n_chunks * ct - seq
    kb_ref[pl.ds(0, seq), :] = k_ref[0].astype(BF16)
    kb_ref[pl.ds(seq, pad), :] = jnp.zeros((pad, PAIR_DIM), BF16)
    for c in range(n_chunks):
        rows = min(ct, seq - c * ct)
        vc, kc = v_ref[0, pl.ds(c * ct, rows), :], k_ref[0, pl.ds(c * ct, rows), :]
        if rows < ct:
            fill = jnp.zeros((ct - rows, PAIR_DIM), F32)
            vc, kc = jnp.concatenate([vc, fill], axis=0), jnp.concatenate([kc, fill], axis=0)
        ones_row = (lax.broadcasted_iota(jnp.int32, (SUM_ROWS, ct), 0) == 0).astype(F32)
        vt_ref[c] = jnp.concatenate([vc.T, ones_row], axis=0).astype(BF16)
        kt_ref[0, :, pl.ds(c * ct, rows)] = kc.T[:, :rows]

    first = lax.broadcasted_iota(jnp.int32, (1, PAIR_DIM), 1) < HEAD_DIM
    n_full, tail = divmod(seq, tq)
    for i in range(pl.cdiv(seq, tq)):
        r0 = i * tq
        n_before = r0 // ct
        if i < n_full:
            t, valid = tq, tq
            qt = q_ref[0, pl.ds(r0, t), :]
        else:
            t, valid = ATTN_TAIL_TILE, tail
            qt = jnp.concatenate([q_ref[0, pl.ds(r0, tail), :], jnp.zeros((t - tail, PAIR_DIM), BF16)], axis=0)
        zero = jnp.zeros_like(qt)
        qq = jnp.concatenate([jnp.where(first, qt, zero), jnp.where(first, zero, qt)], axis=0)
        def scores(c, qq=qq):
            return _scores_t(kb_ref[pl.ds(c * ct, ct), :], qq)

        s = scores(0)
        m = jnp.full((1, 2 * t), NEG_INF, F32)
        acc, p_prev = jnp.zeros((PAIR_DIM + SUM_ROWS, 2 * t), F32), None
        for c in range(n_before + 1):
            s_next = None
            if c == n_before:
                kpos = c * ct + lax.broadcasted_iota(jnp.int32, (ct, 2 * t), 0)
                qpos = r0 + lax.broadcasted_iota(jnp.int32, (ct, 2 * t), 1) % t
                s = jnp.where(kpos <= qpos, s, NEG_INF)
            else:
                s_next = scores(c + 1)
            m, alpha, p = _softmax_t(s, m)
            if p_prev is not None:
                acc = alpha * (acc + jnp.dot(vt_ref[c - 1], p_prev, preferred_element_type=F32))
            s, p_prev = s_next, p
        acc = acc + jnp.dot(vt_ref[n_before], p_prev, preferred_element_type=F32)
        l, acc = acc[PAIR_DIM:PAIR_DIM + 1], acc[:PAIR_DIM]
        ot = acc[:, :t] / l[:, :t] - lam * (acc[:, t:] / l[:, t:])
        ms = jnp.mean(ot * ot, axis=0, keepdims=True)
        ot = ot * lax.rsqrt(ms + EPS) * sgt_ref[...] * (1.0 - lam_init)
        o_ref[0, pl.ds(r0, valid), :] = ot.T[:valid].astype(BF16)


def _attn_prompt(lams, sub_gain, q, k, v, batch, seq, lam_init):
    q3 = q.reshape(batch, seq, D_ATTN)
    k3 = k.reshape(batch, seq, D_ATTN)
    v3 = v.reshape(batch, seq, D_ATTN)
    n_chunks = pl.cdiv(seq, ATTN_CHUNK)
    assert seq % ATTN_TILE <= ATTN_TAIL_TILE and ATTN_CHUNK % ATTN_TILE == 0
    vec = pl.BlockSpec((1, HEAD_DIM), lambda b, h: (0, 0))
    blk = pl.BlockSpec((1, seq, PAIR_DIM), lambda b, h: (b, 0, h))
    out, kt = pl.pallas_call(
        functools.partial(_attn_prompt_kernel, seq=seq, lam_init=lam_init),
        grid=(batch, N_HEADS),
        in_specs=[vec, vec, vec, vec, pl.BlockSpec((PAIR_DIM, 1), lambda b, h: (0, 0)), blk, blk, blk],
        out_specs=[blk, pl.BlockSpec((1, PAIR_DIM, seq), lambda b, h: (b, h, 0))],
        out_shape=[jax.ShapeDtypeStruct((batch, seq, D_ATTN), BF16),
                   jax.ShapeDtypeStruct((batch, D_ATTN, seq), F32)],
        scratch_shapes=[pltpu.VMEM((n_chunks * ATTN_CHUNK, PAIR_DIM), BF16),
                        pltpu.VMEM((n_chunks, PAIR_DIM + SUM_ROWS, ATTN_CHUNK), BF16)],
        compiler_params=_params("parallel", "parallel"),
        name="attn_prompt",
    )(*lams, sub_gain.reshape(PAIR_DIM, 1), q3, k3, v3)
    return out.reshape(batch * seq, D_ATTN), kt.reshape(batch, 2 * N_HEADS, HEAD_DIM, seq)


SAMPLE_ROWS = 8


def _attn_sample_kernel(pt_ref, lq1, lk1, lq2, lk2, sg_ref, qa_ref, qb_ref, kn_ref, vn_ref, *rest,
                        n_q, page, lam_init):
    del pt_ref
    npg = PAGES_PER_STEP
    k_refs, v_refs = rest[:npg], rest[npg:2 * npg]
    o_ref, m_ref, l_ref, acc_ref = rest[2 * npg:]
    s_idx = pl.program_id(1)
    rows = N_HEADS * SAMPLE_ROWS
    half = SAMPLE_ROWS // 2

    @pl.when(s_idx == 0)
    def _():
        m_ref[...] = jnp.full(m_ref.shape, NEG_INF, F32)
        l_ref[...] = jnp.zeros(l_ref.shape, F32)
        acc_ref[...] = jnp.zeros(acc_ref.shape, F32)

    qa = [qa_ref[0, h].astype(BF16) for h in range(N_HEADS)]
    qb = [qb_ref[0, h].astype(BF16) for h in range(N_HEADS)]

    def scores(k_head):
        blocks = []
        for h in range(N_HEADS):
            s1 = jnp.dot(qa[h], k_head(2 * h).astype(BF16), preferred_element_type=F32)
            s2 = jnp.dot(qb[h], k_head(2 * h + 1).astype(BF16), preferred_element_type=F32)
            blocks.append(s1 + s2)
        return jnp.concatenate(blocks, axis=0)

    def update(s, v_heads):
        m = m_ref[...]
        m_new = jnp.maximum(m, jnp.max(s, axis=-1, keepdims=True))
        alpha = jnp.exp2(m - m_new)
        p = jnp.exp2(s - m_new)
        l_ref[...] = alpha * l_ref[...] + jnp.sum(p, axis=-1, keepdims=True)
        pv = []
        for h in range(N_HEADS):
            ph = p[h * SAMPLE_ROWS:(h + 1) * SAMPLE_ROWS]
            acc = None
            for i, v_head in enumerate(v_heads):
                d = jnp.dot(ph[:, i * page:(i + 1) * page].astype(BF16), v_head(h).astype(BF16),
                            preferred_element_type=F32)
                acc = d if acc is None else acc + d
            pv.append(acc)
        acc_ref[...] = alpha * acc_ref[...] + jnp.concatenate(pv, axis=0)
        m_ref[...] = m_new

    def k_slice(ref, lead):
        return lambda h16: ref[(*lead, h16)]

    def v_slice(ref, lead):

        return lambda h: ref[(*lead, pl.ds(h, page, stride=N_HEADS), slice(None))]

    s = jnp.concatenate([scores(k_slice(kr, (0, 0))) for kr in k_refs], axis=1)
    update(s, [v_slice(vr, (0, 0)) for vr in v_refs])

    @pl.when(s_idx == pl.num_programs(1) - 1)
    def _():
        qi = lax.broadcasted_iota(jnp.int32, (rows, page), 0) % half
        kj = lax.broadcasted_iota(jnp.int32, (rows, page), 1)
        s_new = scores(k_slice(kn_ref, (0,)))
        update(jnp.where(kj <= qi, s_new, NEG_INF), [v_slice(vn_ref, (0,))])

        f = acc_ref[...] / l_ref[...]
        lam = _diff_lambda(lq1, lk1, lq2, lk2, lam_init)
        for h in range(N_HEADS):
            r0 = h * SAMPLE_ROWS
            o = f[r0:r0 + SAMPLE_ROWS] - lam * pltpu.roll(f[r0:r0 + SAMPLE_ROWS], half, 0)
            o_ref[0, h] = _sub_ln(o, sg_ref[...], lam_init)


def _attn_sample(page_table, lams, sub_gain, q, k_new, v_new, cache_k, cache_v, layer, lam_init):
    db, n_pages = page_table.shape
    n_q = q.shape[0] // db
    page = cache_k.shape[2]
    npg = PAGES_PER_STEP
    rows = N_HEADS * SAMPLE_ROWS
    half = SAMPLE_ROWS // 2
    assert n_q <= half

    q5 = q.astype(F32).reshape(db, n_q, N_HEADS, 2, HEAD_DIM).transpose(0, 2, 3, 1, 4)
    q5 = jnp.pad(q5, ((0, 0), (0, 0), (0, 0), (0, half - n_q), (0, 0)))
    zeros = jnp.zeros_like(q5[:, :, 0])
    qa = jnp.concatenate([q5[:, :, 0], zeros], axis=2)
    qb = jnp.concatenate([zeros, q5[:, :, 1]], axis=2)
    v_rows = page * N_HEADS
    kn = jnp.pad(k_new.reshape(db, n_q, 2 * N_HEADS, HEAD_DIM).transpose(0, 2, 3, 1),
                 ((0, 0), (0, 0), (0, 0), (0, page - n_q)))
    vn = jnp.pad(v_new.reshape(db, n_q * N_HEADS, PAIR_DIM), ((0, 0), (0, v_rows - n_q * N_HEADS), (0, 0)))
    ck = cache_k.transpose(0, 1, 3, 4, 2)
    cv = cache_v.reshape(cache_v.shape[0], cache_v.shape[1], v_rows, PAIR_DIM)

    vec = pl.BlockSpec((1, HEAD_DIM), lambda b, s, pt: (0, 0))
    qspec = pl.BlockSpec((1, N_HEADS, SAMPLE_ROWS, HEAD_DIM), lambda b, s, pt: (b, 0, 0, 0))
    k_page = (2 * N_HEADS, HEAD_DIM, page)

    def k_spec(p):
        return pl.BlockSpec((1, 1) + k_page, lambda b, s, pt: (layer, pt[b, s * npg + p], 0, 0, 0))

    def v_spec(p):
        return pl.BlockSpec((1, 1, v_rows, PAIR_DIM), lambda b, s, pt: (layer, pt[b, s * npg + p], 0, 0))

    grid_spec = pltpu.PrefetchScalarGridSpec(
        num_scalar_prefetch=1,
        grid=(db, n_pages // npg),
        in_specs=[vec, vec, vec, vec, pl.BlockSpec((1, PAIR_DIM), lambda b, s, pt: (0, 0)), qspec, qspec,
                  pl.BlockSpec((1,) + k_page, lambda b, s, pt: (b, 0, 0, 0)),
                  pl.BlockSpec((1, v_rows, PAIR_DIM), lambda b, s, pt: (b, 0, 0))]
                 + [k_spec(p) for p in range(npg)] + [v_spec(p) for p in range(npg)],
        out_specs=pl.BlockSpec((1, N_HEADS, SAMPLE_ROWS, PAIR_DIM), lambda b, s, pt: (b, 0, 0, 0)),
        scratch_shapes=[pltpu.VMEM((rows, 1), F32), pltpu.VMEM((rows, 1), F32),
                        pltpu.VMEM((rows, PAIR_DIM), F32)],
    )
    out = pl.pallas_call(
        functools.partial(_attn_sample_kernel, n_q=n_q, page=page, lam_init=lam_init),
        grid_spec=grid_spec,
        out_shape=jax.ShapeDtypeStruct((db, N_HEADS, SAMPLE_ROWS, PAIR_DIM), F32),
        compiler_params=_params("parallel", "arbitrary"),
        name="attn_sample",
    )(page_table, *lams, sub_gain, qa, qb, kn, vn, *([ck] * npg), *([cv] * npg))
    return out[:, :, :n_q].transpose(0, 2, 1, 3).reshape(db * n_q, D_ATTN).astype(BF16)


def _ln_silu(cf, g, b):
    mu = jnp.mean(cf, axis=-1, keepdims=True)
    d = cf - mu
    var = jnp.mean(d * d, axis=-1, keepdims=True)
    cn = d * lax.rsqrt(var + EPS) * g + b
    return cn * jax.nn.sigmoid(cn)


def _conv_prompt_kernel(h_ref, w_ref, b_ref, g_ref, bb_ref, o_ref, ext_ref, sh_ref, *, seq):
    lanes = 128
    n_sh = sh_ref.shape[1]
    ext_ref[pl.ds(0, CONV_HALO), :] = jnp.zeros((CONV_HALO, D_CONV), F32)
    ext_ref[pl.ds(CONV_HALO, seq), :] = h_ref[0]
    ext_ref[pl.ds(CONV_HALO + seq, 8), :] = jnp.zeros((8, D_CONV), F32)
    shift0 = CONV_HALO - (CONV_W - 1)
    n_tiles = seq // CONV_ROWS

    for c in range(D_CONV // lanes):
        cols = pl.ds(c * lanes, lanes)
        for r in range(8):
            sh_ref[r] = ext_ref[pl.ds(r, n_sh), cols]

        def tile(it, carry, cols=cols):
            t0 = pl.multiple_of(it * CONV_ROWS, 8)
            acc = jnp.zeros((CONV_ROWS, lanes), F32)
            for tap in range(CONV_W):
                a, r = divmod(tap + shift0, 8)
                acc = acc + sh_ref[r, pl.ds(t0 + 8 * a, CONV_ROWS), :] * w_ref[pl.ds(tap, 1), cols]
            ext_ref[pl.ds(CONV_HALO + t0, CONV_ROWS), cols] = acc
            return carry

        lax.fori_loop(0, n_tiles, tile, 0, unroll=2)

    def ln_tile(it, carry):
        t0 = pl.multiple_of(it * CONV_ROWS, 8)
        cf = ext_ref[pl.ds(CONV_HALO + t0, CONV_ROWS), :] + b_ref[...]
        o_ref[0, pl.ds(t0, CONV_ROWS), :] = _ln_silu(cf, g_ref[...], bb_ref[...]).astype(BF16)
        return carry

    lax.fori_loop(0, n_tiles, ln_tile, 0, unroll=4)


def _conv_prompt(h, dw_w, dw_b, ln_g, ln_b, batch, seq):
    h3 = h.reshape(batch, seq, D_CONV)
    row = pl.BlockSpec((1, D_CONV), lambda b: (0, 0))
    out = pl.pallas_call(
        functools.partial(_conv_prompt_kernel, seq=seq),
        grid=(batch,),
        in_specs=[pl.BlockSpec((1, seq, D_CONV), lambda b: (b, 0, 0)),
                  pl.BlockSpec((CONV_W, D_CONV), lambda b: (0, 0)), row, row, row],
        out_specs=pl.BlockSpec((1, seq, D_CONV), lambda b: (b, 0, 0)),
        out_shape=jax.ShapeDtypeStruct((batch, seq, D_CONV), BF16),
        scratch_shapes=[pltpu.VMEM((CONV_HALO + seq + 8, D_CONV), F32),
                        pltpu.VMEM((8, CONV_HALO + seq, 128), F32)],
        compiler_params=_params("parallel"),
        name="conv_prompt",
    )(h3, dw_w, dw_b, ln_g, ln_b)
    return out.reshape(batch * seq, D_CONV)


def _conv_sample_kernel(ext_ref, w_ref, b_ref, g_ref, bb_ref, o_ref, *, n_q):
    w = w_ref[...]
    for t in range(n_q):
        cf = jnp.sum(ext_ref[0, pl.ds(t, CONV_W), :] * w, axis=0, keepdims=True) + b_ref[...]
        o_ref[0, pl.ds(t, 1), :] = _ln_silu(cf, g_ref[...], bb_ref[...])


def _conv_sample(ext, dw_w, dw_b, ln_g, ln_b, n_q):
    db, rows, _ = ext.shape
    row = pl.BlockSpec((1, D_CONV), lambda b: (0, 0))
    out = pl.pallas_call(
        functools.partial(_conv_sample_kernel, n_q=n_q),
        grid=(db,),
        in_specs=[pl.BlockSpec((1, rows, D_CONV), lambda b: (b, 0, 0)),
                  pl.BlockSpec((CONV_W, D_CONV), lambda b: (0, 0)), row, row, row],
        out_specs=pl.BlockSpec((1, n_q, D_CONV), lambda b: (b, 0, 0)),
        out_shape=jax.ShapeDtypeStruct((db, n_q, D_CONV), F32),
        compiler_params=_params("parallel"),
        name="conv_sample",
    )(ext, dw_w, dw_b, ln_g, ln_b)
    return out.reshape(db * n_q, D_CONV).astype(BF16)


def _out_proj_kernel(x_ref, oa_ref, c_ref, wa_ref, wc_ref, o_ref):
    o_ref[...] = (x_ref[...]
                  + jnp.dot(oa_ref[...], wa_ref[...], preferred_element_type=F32)
                  + jnp.dot(c_ref[...], wc_ref[...], preferred_element_type=F32))


def _out_proj(x, oa, c, w_bf16, layer, tm):
    m = x.shape[0]
    tn = D_MODEL // 2
    return pl.pallas_call(
        _out_proj_kernel,
        grid=(m // tm, D_MODEL // tn),
        in_specs=[pl.BlockSpec((tm, tn), lambda i, j: (i, j)),
                  pl.BlockSpec((tm, D_ATTN), lambda i, j: (i, 0)),
                  pl.BlockSpec((tm, D_CONV), lambda i, j: (i, 0)),
                  pl.BlockSpec((None, D_ATTN, tn), lambda i, j: (layer, 0, j)),
                  pl.BlockSpec((None, D_CONV, tn), lambda i, j: (layer, D_ATTN // D_CONV, j))],
        out_specs=pl.BlockSpec((tm, tn), lambda i, j: (i, j)),
        out_shape=jax.ShapeDtypeStruct((m, D_MODEL), F32),
        compiler_params=_params("parallel", "arbitrary"),
        name="out_proj",
    )(x, oa, c, w_bf16, w_bf16)


def _ffn_kernel(x_ref, g_ref, wu_ref, wd_ref, o_ref, xn_ref):
    @pl.when(pl.program_id(1) == 0)
    def _():
        x = x_ref[...]
        xn_ref[...] = _rms_rows(x, g_ref[...]).astype(BF16)
        o_ref[...] = x

    h = jnp.dot(xn_ref[...], wu_ref[...], preferred_element_type=F32)
    a = jnp.square(jnp.maximum(h, 0.0)).astype(BF16)
    o_ref[...] += jnp.dot(a, wd_ref[...], preferred_element_type=F32)


def _ffn(x, gain, wu_bf16, wd_bf16, layer, tm, tf):
    m = x.shape[0]
    return pl.pallas_call(
        _ffn_kernel,
        grid=(m // tm, D_FF // tf),
        in_specs=[pl.BlockSpec((tm, D_MODEL), lambda i, j: (i, 0)),
                  pl.BlockSpec((1, D_MODEL), lambda i, j: (0, 0)),
                  pl.BlockSpec((None, D_MODEL, tf), lambda i, j: (layer, 0, j)),
                  pl.BlockSpec((None, tf, D_MODEL), lambda i, j: (layer, j, 0))],
        out_specs=pl.BlockSpec((tm, D_MODEL), lambda i, j: (i, 0)),
        out_shape=jax.ShapeDtypeStruct((m, D_MODEL), F32),
        scratch_shapes=[pltpu.VMEM((tm, D_MODEL), BF16)],
        compiler_params=_params("parallel", "arbitrary"),
        name="ffn",
    )(x, gain, wu_bf16, wd_bf16)


def _row_tile(m):
    for tm in (1032, 1024, 688, 512, 344, 256, 128, 64, 32, 16, 8):
        if m % tm == 0:
            return tm
    raise ValueError(f"unsupported row count {m}")


def kernel(x_prompt, x_sample, cache_k, cache_v, state_conv, page_table, meta_tokens, attn_norm, w_in,
           q_norm, k_norm, lambda_q1, lambda_k1, lambda_q2, lambda_k2, attn_subln, dw_w, dw_b,
           conv_ln_g, conv_ln_b, w_out, mlp_norm, w_up, w_down):
    depth = w_in.shape[0]
    b, seq_real, _ = x_prompt.shape
    seq = N_META + seq_real
    db, ds, _ = x_sample.shape

    w_in_b, w_out_b = w_in.astype(BF16), w_out.astype(BF16)
    w_up_b, w_down_b = w_up.astype(BF16), w_down.astype(BF16)
    heads_per_chunk = IN_CHUNK // HEAD_DIM
    blk = jnp.arange(IN_CHUNK) // HEAD_DIM
    gm = jnp.where(blk[:, None] == blk[None, :], 1.0 / HEAD_DIM, 0.0).astype(BF16)

    meta = jnp.broadcast_to(meta_tokens.astype(x_prompt.dtype)[None], (b, N_META, D_MODEL))
    xp = jnp.concatenate([meta, x_prompt], axis=1).reshape(b * seq, D_MODEL)
    xs = x_sample.reshape(db * ds, D_MODEL)
    tm_p, tm_s = _row_tile(b * seq), _row_tile(db * ds)

    kp, vp, cp, ks_, vs_, cs_ = [], [], [], [], [], []
    for l in range(depth):
        lam0 = lambda_init(l)
        lams = [a[l][None, :] for a in (lambda_q1, lambda_k1, lambda_q2, lambda_k2)]
        sub_gain = attn_subln[l][None, :]
        an, mn = attn_norm[l][None, :], mlp_norm[l][None, :]
        qg = jnp.tile(q_norm[l], heads_per_chunk)[None, :]
        kg = jnp.tile(k_norm[l], heads_per_chunk)[None, :]
        cw, cb = dw_w[l], dw_b[l][None, :]
        lg, lb = conv_ln_g[l][None, :], conv_ln_b[l][None, :]

        q, k, v, h = _in_proj(xp, an, w_in_b, l, gm, qg, kg, tm_p)
        oa, kt = _attn_prompt(lams, sub_gain, q, k, v, b, seq, lam0)
        c = _conv_prompt(h, cw, cb, lg, lb, b, seq)
        x1 = _out_proj(xp, oa, c, w_out_b, l, tm_p)
        xp = _ffn(x1, mn, w_up_b, w_down_b, l, tm_p, FFN_CHUNK)
        kp.append(kt)
        vp.append(v.reshape(b, seq, N_HEADS, PAIR_DIM))
        cp.append(h.reshape(b, seq, D_CONV)[:, seq - (CONV_W - 1):])

        q, k, v, h = _in_proj(xs, an, w_in_b, l, gm, qg, kg, tm_s)
        oa = _attn_sample(page_table, lams, sub_gain, q, k, v, cache_k, cache_v, l, lam0)
        ext = jnp.concatenate([state_conv[l], h.reshape(db, ds, D_CONV)], axis=1)
        c = _conv_sample(ext, cw, cb, lg, lb, ds)
        x1 = _out_proj(xs, oa, c, w_out_b, l, tm_s)
        xs = _ffn(x1, mn, w_up_b, w_down_b, l, tm_s, FFN_CHUNK)
        ks_.append(k.reshape(db, ds, 2 * N_HEADS, HEAD_DIM))
        vs_.append(v.reshape(db, ds, N_HEADS, PAIR_DIM))
        cs_.append(ext[:, ds:])

    y_prompt = xp.reshape(b, seq, D_MODEL)[:, N_META:]
    y_sample = xs.reshape(db, ds, D_MODEL)
    return (y_prompt, y_sample, jnp.stack(kp).transpose(0, 1, 4, 2, 3), jnp.stack(vp), jnp.stack(cp),
            jnp.stack(ks_), jnp.stack(vs_), jnp.stack(cs_))
```
